```python
import math
import jax, jax.numpy as jnp
from jax import lax
import numpy as np

D_MODEL = 2048
BATCH = 4
SEQ = 2048
DEPTH = 4
DEC_BATCH = 32
DEC_SEQ = 8
PAST_LEN = 16384
PAGE_SIZE = 128

N_A = DEPTH // 2
N_B = DEPTH - N_A
RET_HEADS = 8
RET_DK = D_MODEL // RET_HEADS
RET_DV = 2 * D_MODEL // RET_HEADS
RET_CHUNK = 128
RET_ROPE_BASE = 10000.0
N_Q_HEADS = 32
N_KV_HEADS = 4
HEAD_DIM = D_MODEL // N_Q_HEADS
GQA_GROUPS = N_Q_HEADS // N_KV_HEADS
ROPE_DIM = HEAD_DIM // 4
ROPE_THETA = 500000.0
WINDOW = 128
ATT_BLOCK = 128
D_FF = 2 * D_MODEL
CONV_W = 3
N_MOD = 6
EPS = 1e-6
NEG_INF = -1e30

kernel_name = 'yoco_retention_swa_sink_convffn_step'


def rmsnorm(x, g):
    xf = x.astype(jnp.float32)
    y = xf * lax.rsqrt(jnp.mean(xf * xf, axis=-1, keepdims=True) + EPS)
    return (y * g.astype(jnp.float32)).astype(x.dtype)


def modulate(h, shift, scale):
    return h * (1 + scale[:, None, :]) + shift[:, None, :]


def rope(x, pos, dim, base):
    half = dim // 2
    inv = base ** (-jnp.arange(half, dtype=jnp.float32) * 2.0 / dim)
    ang = pos.astype(jnp.float32)[:, None] * inv[None, :]
    cos = jnp.cos(ang)[:, None, :]
    sin = jnp.sin(ang)[:, None, :]
    xr = x[..., :dim].astype(jnp.float32)
    x1, x2 = xr[..., :half], xr[..., half:]
    rot = jnp.concatenate([x1 * cos - x2 * sin, x2 * cos + x1 * sin], axis=-1).astype(x.dtype)
    return jnp.concatenate([rot, x[..., dim:]], axis=-1)


def retention_log_decay():
    return jnp.log(1.0 - jnp.exp2(-5.0 - jnp.arange(RET_HEADS, dtype=jnp.float32)))


def retention_chunk(q, k, v, s_prev, log_gamma):
    L = q.shape[1]
    idx = jnp.arange(L, dtype=jnp.float32)
    rel = idx[:, None] - idx[None, :]
    decay = jnp.where(rel >= 0, jnp.exp(jnp.maximum(rel, 0.0)[None] * log_gamma[:, None, None]), 0.0)
    scores = jnp.einsum('blhd,bmhd->bhlm', q, k) * decay[None]
    o = jnp.einsum('bhlm,bmhe->blhe', scores, v)
    q_decay = jnp.exp((idx + 1.0)[:, None] * log_gamma[None, :])
    o = o + jnp.einsum('blhd,bhde->blhe', q, s_prev) * q_decay[None, :, :, None]
    k_decay = jnp.exp((L - 1.0 - idx)[:, None] * log_gamma[None, :])
    s_new = (jnp.exp(L * log_gamma)[None, :, None, None] * s_prev
             + jnp.einsum('blhd,blhe->bhde', k * k_decay[None, :, :, None], v))
    return o, s_new


def retention_prompt(q, k, v, log_gamma):
    B, T = q.shape[:2]
    n = T // RET_CHUNK

    def to_chunks(a):
        return a.reshape(B, n, RET_CHUNK, *a.shape[2:]).swapaxes(0, 1)

    def step(s, qkv):
        o, s_new = retention_chunk(qkv[0], qkv[1], qkv[2], s, log_gamma)
        return s_new, o

    s0 = jnp.zeros((B, RET_HEADS, RET_DK, RET_DV), jnp.float32)
    s_fin, o = lax.scan(step, s0, (to_chunks(q), to_chunks(k), to_chunks(v)))
    return o.swapaxes(0, 1).reshape(B, T, RET_HEADS, RET_DV), s_fin


def retention_mixer(h, pos, w_in, gn_g, w_out, log_gamma, s_prev):
    B, T, _ = h.shape
    qk = RET_HEADS * RET_DK
    vd = RET_HEADS * RET_DV
    q, k, v, g = jnp.split(h @ w_in, [qk, 2 * qk, 2 * qk + vd], axis=-1)
    q = rope(q.reshape(B, T, RET_HEADS, RET_DK), pos, RET_DK, RET_ROPE_BASE).astype(jnp.float32)
    k = rope(k.reshape(B, T, RET_HEADS, RET_DK), pos, RET_DK, RET_ROPE_BASE).astype(jnp.float32) * RET_DK ** -0.5
    v = v.reshape(B, T, RET_HEADS, RET_DV).astype(jnp.float32)
    if s_prev is None:
        o, s_new = retention_prompt(q, k, v, log_gamma)
    else:
        o, s_new = retention_chunk(q, k, v, s_prev.astype(jnp.float32), log_gamma)
    mu = jnp.mean(o, axis=-1, keepdims=True)
    var = jnp.mean(jnp.square(o - mu), axis=-1, keepdims=True)
    o = ((o - mu) * lax.rsqrt(var + EPS)).reshape(B, T, vd) * gn_g.astype(jnp.float32)
    y = (jax.nn.silu(g.astype(jnp.float32)) * o).astype(h.dtype) @ w_out
    return y, s_new


def sink_attention(q, k, v, q_pos, k_pos, sinks):
    lead = q.shape[:-3]
    tq = q.shape[-3]
    qg = q.reshape(*lead, tq, N_KV_HEADS, GQA_GROUPS, HEAD_DIM)
    s = jnp.einsum('...qkgd,...skd->...kgqs', qg, k, preferred_element_type=jnp.float32) * HEAD_DIM ** -0.5
    rel = q_pos[..., :, None] - k_pos[..., None, :]
    ok = (rel >= 0) & (rel <= WINDOW) & (k_pos[..., None, :] >= 0)
    s = jnp.where(ok[..., None, None, :, :], s, NEG_INF)
    sink = sinks.astype(jnp.float32).reshape(N_KV_HEADS, GQA_GROUPS, 1, 1)
    m = jnp.maximum(jnp.max(s, axis=-1, keepdims=True), sink)
    p = jnp.exp(s - m)
    p = (p / (jnp.sum(p, axis=-1, keepdims=True) + jnp.exp(sink - m))).astype(v.dtype)
    o = jnp.einsum('...kgqs,...skd->...qkgd', p, v)
    return o.reshape(*lead, tq, N_Q_HEADS * HEAD_DIM)


def swa_prompt(q, k, v, sinks):
    B, T = q.shape[:2]
    nb = T // ATT_BLOCK
    qb = q.reshape(B, nb, ATT_BLOCK, N_Q_HEADS, HEAD_DIM)

    def with_prev(a):
        a = a.reshape(B, nb, ATT_BLOCK, N_KV_HEADS, HEAD_DIM)
        prev = jnp.pad(a, ((0, 0), (1, 0), (0, 0), (0, 0), (0, 0)))[:, :-1]
        return jnp.concatenate([prev, a], axis=2)

    pos = jnp.arange(T, dtype=jnp.int32).reshape(nb, ATT_BLOCK)
    k_pos = jnp.concatenate([pos - ATT_BLOCK, pos], axis=1)
    o = sink_attention(qb, with_prev(k), with_prev(v), pos, k_pos, sinks)
    return o.reshape(B, T, N_Q_HEADS * HEAD_DIM)


def conv_ffn(h, w_up, conv_w, conv_b, w_down, conv_state):
    u = h @ w_up
    B, T, C = u.shape
    past = jnp.zeros((B, CONV_W - 1, C), u.dtype) if conv_state is None else conv_state.astype(u.dtype)
    ext = jnp.concatenate([past, u], axis=1)
    acc = conv_b
    for i in range(CONV_W):
        acc = acc + conv_w[i] * ext[:, i:i + T]
    a, b = jnp.split(acc, 2, axis=-1)
    y = (jax.nn.silu(a) * b) @ w_down
    return y, ext[:, -(CONV_W - 1):]


def setup_inputs(seed: int = 0) -> dict:
    key = jax.random.key(seed)
    ks = iter(jax.random.split(key, 40))

    def nrm(shape, scale):
        return jax.random.normal(next(ks), shape, jnp.float32) * scale

    f2 = 2 * D_FF
    win_buf = min(WINDOW, PAST_LEN)
    d = D_MODEL
    return {
        'x_prompt': nrm((BATCH, SEQ, d), 1.0),
        'x_sample': nrm((DEC_BATCH, DEC_SEQ, d), 1.0),
        'state_ret': nrm((N_A, DEC_BATCH, RET_HEADS, RET_DK, RET_DV), 0.05),
        'cache_win_k': nrm((DEC_BATCH, win_buf, N_KV_HEADS, HEAD_DIM), 1.0),
        'cache_win_v': nrm((DEC_BATCH, win_buf, N_KV_HEADS, HEAD_DIM), 1.0),
        'state_conv': nrm((DEPTH, DEC_BATCH, CONV_W - 1, f2), 1.0),
        'c_prompt': nrm((BATCH, d), 1.0),
        'c_sample': nrm((DEC_BATCH, d), 1.0),
        'w_ada': nrm((DEPTH, d, N_MOD * d), 0.5 * d ** -0.5),
        'b_ada': nrm((DEPTH, N_MOD * d), 0.02),
        'norm_mix': 1.0 + nrm((DEPTH, d), 0.05),
        'norm_ffn': 1.0 + nrm((DEPTH, d), 0.05),
        'ret_w_in': nrm((N_A, d, 2 * RET_HEADS * RET_DK + 2 * RET_HEADS * RET_DV), d ** -0.5),
        'ret_gn': 1.0 + nrm((N_A, RET_HEADS * RET_DV), 0.05),
        'ret_w_out': nrm((N_A, RET_HEADS * RET_DV, d), (RET_HEADS * RET_DV) ** -0.5),
        'kv_norm': 1.0 + nrm((d,), 0.05),
        'kv_w_ada': nrm((d, 2 * d), 0.5 * d ** -0.5),
        'kv_b_ada': nrm((2 * d,), 0.02),
        'w_kv': nrm((d, 2 * N_KV_HEADS * HEAD_DIM), d ** -0.5),
        'att_w_q': nrm((N_B, d, N_Q_HEADS * HEAD_DIM), d ** -0.5),
        'att_sinks': nrm((N_B, N_Q_HEADS), 1.0),
        'att_w_o': nrm((N_B, N_Q_HEADS * HEAD_DIM, d), (N_Q_HEADS * HEAD_DIM) ** -0.5),
        'ffn_w_up': nrm((DEPTH, d, f2), d ** -0.5),
        'ffn_conv_w': nrm((DEPTH, CONV_W, f2), CONV_W ** -0.5),
        'ffn_conv_b': nrm((DEPTH, f2), 0.02),
        'ffn_w_down': nrm((DEPTH, D_FF, d), D_FF ** -0.5),
        'norm_f': 1.0 + nrm((d,), 0.05),
    }


def reference(x_prompt, x_sample, state_ret, cache_win_k, cache_win_v, state_conv, c_prompt, c_sample,
              w_ada, b_ada, norm_mix, norm_ffn, ret_w_in, ret_gn, ret_w_out,
              kv_norm, kv_w_ada, kv_b_ada, w_kv, att_w_q, att_sinks, att_w_o,
              ffn_w_up, ffn_conv_w, ffn_conv_b, ffn_w_down, norm_f):
    log_gamma = retention_log_decay()

    def trunk(x, c, pos, ret_in, win_k_in, win_v_in, conv_in):
        sample = ret_in is not None
        B, T, _ = x.shape
        cs = jax.nn.silu(c)
        ret_out, conv_out = [], []
        k_sh = v_sh = k_pos = win_k_out = win_v_out = None
        for l in range(DEPTH):
            if l == N_A:
                kv_shift, kv_scale = jnp.split(cs @ kv_w_ada + kv_b_ada, 2, axis=-1)
                hk = modulate(rmsnorm(x, kv_norm), kv_shift, kv_scale)
                k_new, v_new = jnp.split(hk @ w_kv, 2, axis=-1)
                k_new = rope(k_new.reshape(B, T, N_KV_HEADS, HEAD_DIM), pos, ROPE_DIM, ROPE_THETA)
                v_new = v_new.reshape(B, T, N_KV_HEADS, HEAD_DIM)
                if sample:
                    wb = win_k_in.shape[1]
                    k_sh = jnp.concatenate([win_k_in.astype(k_new.dtype), k_new], axis=1)
                    v_sh = jnp.concatenate([win_v_in.astype(v_new.dtype), v_new], axis=1)
                    k_pos = PAST_LEN - wb + jnp.arange(wb + T, dtype=jnp.int32)
                    win_k_out, win_v_out = k_sh[:, -wb:], v_sh[:, -wb:]
                else:
                    k_sh, v_sh = k_new, v_new
                    wp = min(WINDOW, T)
                    win_k_out, win_v_out = k_new[:, -wp:], v_new[:, -wp:]
            sh1, sc1, g1, sh2, sc2, g2 = jnp.split(cs @ w_ada[l] + b_ada[l], N_MOD, axis=-1)
            h = modulate(rmsnorm(x, norm_mix[l]), sh1, sc1)
            if l < N_A:
                y, s_new = retention_mixer(h, pos, ret_w_in[l], ret_gn[l], ret_w_out[l], log_gamma,
                                           ret_in[l] if sample else None)
                ret_out.append(s_new)
            else:
                j = l - N_A
                q = rope((h @ att_w_q[j]).reshape(B, T, N_Q_HEADS, HEAD_DIM), pos, ROPE_DIM, ROPE_THETA)
                if sample:
                    o = sink_attention(q, k_sh, v_sh, pos, k_pos, att_sinks[j])
                else:
                    o = swa_prompt(q, k_sh, v_sh, att_sinks[j])
                y = o @ att_w_o[j]
            x = x + g1[:, None, :] * y
            h = modulate(rmsnorm(x, norm_ffn[l]), sh2, sc2)
            y, c_state = conv_ffn(h, ffn_w_up[l], ffn_conv_w[l], ffn_conv_b[l], ffn_w_down[l],
                                  conv_in[l] if sample else None)
            conv_out.append(c_state)
            x = x + g2[:, None, :] * y
        return rmsnorm(x, norm_f), jnp.stack(ret_out), win_k_out, win_v_out, jnp.stack(conv_out)

    pos_p = jnp.arange(x_prompt.shape[1], dtype=jnp.int32)
    pos_s = PAST_LEN + jnp.arange(x_sample.shape[1], dtype=jnp.int32)
    y_prompt, ret_prompt, win_k_prompt, win_v_prompt, conv_prompt = trunk(
        x_prompt, c_prompt, pos_p, None, None, None, None)
    y_sample, ret_sample, win_k_sample, win_v_sample, conv_sample = trunk(
        x_sample, c_sample, pos_s, state_ret, cache_win_k, cache_win_v, state_conv)
    return (y_prompt, y_sample, ret_prompt, ret_sample, win_k_prompt, win_v_prompt,
            win_k_sample, win_v_sample, conv_prompt, conv_sample)
```

```python
import functools

import jax
import jax.numpy as jnp
from jax import lax
from jax.experimental import pallas as pl
from jax.experimental.pallas import tpu as pltpu

F32 = jnp.float32
BF16 = jnp.bfloat16

D_MODEL = 2048
DEPTH = 4
PAST_LEN = 16384
N_A = DEPTH // 2
RET_HEADS = 8
RET_DK = D_MODEL // RET_HEADS
RET_DV = 2 * D_MODEL // RET_HEADS
RET_ROPE_BASE = 10000.0
N_Q_HEADS = 32
N_KV_HEADS = 4
HEAD_DIM = D_MODEL // N_Q_HEADS
GQA_GROUPS = N_Q_HEADS // N_KV_HEADS
ROPE_DIM = HEAD_DIM // 4
ROPE_THETA = 500000.0
WINDOW = 128
ATT_BLOCK = 128
D_FF = 2 * D_MODEL
CONV_W = 3
N_MOD = 6
EPS = 1e-6
NEG_INF = -1e30

LANES = 128
HALO = 16
RET_BLOCK = 256
RET_PAD = 16
VMEM_LIMIT = 56 * 1024 * 1024


def _params(sem):
    return pltpu.CompilerParams(dimension_semantics=sem, vmem_limit_bytes=VMEM_LIMIT)


def _silu(x):
    return x * jax.nn.sigmoid(x)


def _bdot(a, b):
    return jnp.dot(a.astype(BF16), b.astype(BF16), preferred_element_type=F32)


def _bdot_nt(a, b):
    return lax.dot_general(a.astype(BF16), b.astype(BF16), (((1,), (1,)), ((), ())), preferred_element_type=F32)


def _norm_mod(x, g, sc, sh):
    ms = jnp.mean(x * x, axis=-1, keepdims=True)
    y = x * lax.rsqrt(ms + EPS) * g
    return y * (1.0 + sc) + sh


def _ada_kernel(c_ref, w_ref, b_ref, o_ref):
    o_ref[0] = _bdot(_silu(c_ref[...]), w_ref[0]) + b_ref[0]


def ada_mods(c_all, w, b, tn=1024):
    L, D, N = w.shape
    R = c_all.shape[0]
    return pl.pallas_call(
        _ada_kernel,
        grid=(L, N // tn),
        in_specs=[pl.BlockSpec((R, D), lambda l, j: (0, 0)),
                  pl.BlockSpec((1, D, tn), lambda l, j: (l, 0, j)),
                  pl.BlockSpec((1, 1, tn), lambda l, j: (l, 0, j))],
        out_specs=pl.BlockSpec((1, R, tn), lambda l, j: (l, 0, j)),
        out_shape=jax.ShapeDtypeStruct((L, R, N), F32),
        compiler_params=_params(("arbitrary", "arbitrary")),
        name="ada_mods",
    )(c_all, w, b.reshape(L, 1, N))


def _store_normed(h_scr, row0, x_ref, g_ref, sc_ref, sh_ref, rows):
    per_row = sc_ref.shape[1] != 1
    g = g_ref[...]

    def body(r, carry):
        r0 = pl.multiple_of(r * 16, 16)
        sc = sc_ref[0, pl.ds(r0, 16), :] if per_row else sc_ref[0]
        sh = sh_ref[0, pl.ds(r0, 16), :] if per_row else sh_ref[0]
        h = _norm_mod(x_ref[0, pl.ds(r0, 16), :], g, sc, sh)
        h_scr[pl.ds(pl.multiple_of(row0 + r0, 16), 16), :] = h.astype(BF16)
        return carry

    lax.fori_loop(0, rows // 16, body, 0)


def _rope64(a, c, s1, s2):
    return a * c + pltpu.roll(a, LANES - ROPE_DIM // 2, axis=1) * s1 + pltpu.roll(a, ROPE_DIM // 2, axis=1) * s2


def _proj_kernel(*refs, tm, tn, rope_cols):
    if rope_cols:
        x_ref, g_ref, sh_ref, sc_ref, w_ref, c_ref, s1_ref, s2_ref, o_ref, h_scr = refs
    else:
        x_ref, g_ref, sh_ref, sc_ref, w_ref, o_ref, h_scr = refs

    @pl.when(pl.program_id(2) == 0)
    def _():
        _store_normed(h_scr, 0, x_ref, g_ref, sc_ref, sh_ref, tm)

    acc = _bdot(h_scr[...], w_ref[...])
    for c in range(tn // LANES):
        a = acc[:, c * LANES:(c + 1) * LANES]
        if c * LANES < rope_cols:
            a = _rope64(a, c_ref[...], s1_ref[...], s2_ref[...])
        o_ref[0, :, c * LANES:(c + 1) * LANES] = a.astype(o_ref.dtype)


def proj(x, g, sh, sc, w, *, tm, tn, out_dtype, rope_tabs=None, rope_cols=0):
    B, T, D = x.shape
    N = w.shape[1]
    R = sh.shape[1]
    assert rope_cols == 0 or rope_cols == N or tn == N
    if R == 1:
        mod_spec = pl.BlockSpec((1, 1, D), lambda b, i, j: (b, 0, 0))
    else:
        mod_spec = pl.BlockSpec((1, tm, D), lambda b, i, j: (b, i, 0))
    in_specs = [pl.BlockSpec((1, tm, D), lambda b, i, j: (b, i, 0)),
                pl.BlockSpec((1, D), lambda b, i, j: (0, 0)),
                mod_spec, mod_spec,
                pl.BlockSpec((D, tn), lambda b, i, j: (0, j))]
    args = [x, g.reshape(1, D), sh, sc, w]
    if rope_cols:
        in_specs += [pl.BlockSpec((tm, LANES), lambda b, i, j: (i, 0))] * 3
        args += list(rope_tabs)
    return pl.pallas_call(
        functools.partial(_proj_kernel, tm=tm, tn=tn, rope_cols=rope_cols),
        grid=(B, T // tm, N // tn),
        in_specs=in_specs,
        out_specs=pl.BlockSpec((1, tm, tn), lambda b, i, j: (b, i, j)),
        out_shape=jax.ShapeDtypeStruct((B, T, N), out_dtype),
        scratch_shapes=[pltpu.VMEM((tm, D), BF16)],
        compiler_params=_params(("arbitrary", "arbitrary", "arbitrary")),
        name="proj",
    )(*args)


def _conv_gate(ua, ub, cwa_ref, cwb_ref, cba_ref, cbb_ref):
    def conv(u, cw_ref, cb_ref):
        acc = cb_ref[...] + cw_ref[0:1, :] * u[0]
        acc = acc + cw_ref[1:2, :] * u[1]
        return acc + cw_ref[2:3, :] * u[2]

    return _silu(conv(ua, cwa_ref, cba_ref)) * conv(ub, cwb_ref, cbb_ref)


def _upconv_kernel(x_ref, xh_ref, g_ref, sh_ref, sc_ref, wa_ref, wb_ref, cwa_ref, cwb_ref, cba_ref, cbb_ref,
                   z_ref, st_ref, h_scr, *, tm):
    i = pl.program_id(1)

    @pl.when(pl.program_id(2) == 0)
    def _():
        _store_normed(h_scr, HALO, x_ref, g_ref, sc_ref, sh_ref, tm)
        hh = _norm_mod(xh_ref[0], g_ref[...], sc_ref[0], sh_ref[0])
        h_scr[0:HALO, :] = jnp.where(i > 0, hh, 0.0).astype(BF16)

    h = h_scr[...]

    def taps(w_ref):
        u = _bdot(h, w_ref[...])
        return (pltpu.roll(u, 2, axis=0)[HALO:], pltpu.roll(u, 1, axis=0)[HALO:], u[HALO:])

    ua = taps(wa_ref)
    ub = taps(wb_ref)
    z_ref[0] = _conv_gate(ua, ub, cwa_ref, cwb_ref, cba_ref, cbb_ref).astype(z_ref.dtype)
    st_ref[0, 0, 0] = ua[2][tm - 8:, :]
    st_ref[0, 0, 1] = ub[2][tm - 8:, :]


def upconv_prompt(x, g, sh, sc, w_up, conv_w, conv_b, *, tm=1024, tn=512):
    B, T, D = x.shape
    F = w_up.shape[1] // 2
    nj = F // tn
    row = lambda b, i, j: (b, i, 0)
    in_specs = [pl.BlockSpec((1, tm, D), row),
                pl.BlockSpec((1, HALO, D), lambda b, i, j: (b, jnp.maximum(i * (tm // HALO) - 1, 0), 0)),
                pl.BlockSpec((1, D), lambda b, i, j: (0, 0)),
                pl.BlockSpec((1, 1, D), lambda b, i, j: (b, 0, 0)),
                pl.BlockSpec((1, 1, D), lambda b, i, j: (b, 0, 0)),
                pl.BlockSpec((D, tn), lambda b, i, j: (0, j)),
                pl.BlockSpec((D, tn), lambda b, i, j: (0, j + nj)),
                pl.BlockSpec((CONV_W, tn), lambda b, i, j: (0, j)),
                pl.BlockSpec((CONV_W, tn), lambda b, i, j: (0, j + nj)),
                pl.BlockSpec((1, tn), lambda b, i, j: (0, j)),
                pl.BlockSpec((1, tn), lambda b, i, j: (0, j + nj))]
    return pl.pallas_call(
        functools.partial(_upconv_kernel, tm=tm),
        grid=(B, T // tm, nj),
        in_specs=in_specs,
        out_specs=[pl.BlockSpec((1, tm, tn), lambda b, i, j: (b, i, j)),
                   pl.BlockSpec((1, 1, 2, 8, tn), lambda b, i, j: (b, i, 0, 0, j))],
        out_shape=[jax.ShapeDtypeStruct((B, T, F), BF16), jax.ShapeDtypeStruct((B, T // tm, 2, 8, F), F32)],
        scratch_shapes=[pltpu.VMEM((HALO + tm, D), BF16)],
        compiler_params=_params(("arbitrary", "arbitrary", "arbitrary")),
        name="upconv_prompt",
    )(x, x, g.reshape(1, D), sh, sc, w_up, w_up, conv_w, conv_w, conv_b.reshape(1, -1), conv_b.reshape(1, -1))


def _upconv_step_kernel(x_ref, g_ref, sh_ref, sc_ref, wa_ref, wb_ref, cwa_ref, cwb_ref, cba_ref, cbb_ref,
                        sta_ref, stb_ref, z_ref, ua_ref, ub_ref, h_scr, *, rows, seq):
    @pl.when(pl.program_id(0) == 0)
    def _():
        _store_normed(h_scr, 0, x_ref, g_ref, sc_ref, sh_ref, rows)

    h = h_scr[...]
    nb = rows // seq

    def taps(w_ref, st_ref, u_ref):
        u = _bdot(h, w_ref[...])
        u_ref[...] = u
        u3 = u.reshape(nb, seq, u.shape[1])
        t = lax.broadcasted_iota(jnp.int32, u3.shape, 1)
        st0 = st_ref[:, 0:1, :]
        st1 = st_ref[:, 1:2, :]
        um1 = jnp.where(t == 0, st1, pltpu.roll(u3, 1, axis=1))
        um2 = jnp.where(t == 0, st0, jnp.where(t == 1, st1, pltpu.roll(u3, 2, axis=1)))
        return (um2, um1, u3)

    ua = taps(wa_ref, sta_ref, ua_ref)
    ub = taps(wb_ref, stb_ref, ub_ref)
    z = _conv_gate(ua, ub, cwa_ref, cwb_ref, cba_ref, cbb_ref)
    z_ref[...] = z.reshape(rows, z.shape[2]).astype(z_ref.dtype)


def upconv_step(x, g, sh, sc, w_up, conv_w, conv_b, state, *, seq, tn=512):
    _, rows, D = x.shape
    F = w_up.shape[1] // 2
    nj = F // tn
    nb = rows // seq
    in_specs = [pl.BlockSpec((1, rows, D), lambda j: (0, 0, 0)),
                pl.BlockSpec((1, D), lambda j: (0, 0)),
                pl.BlockSpec((1, rows, D), lambda j: (0, 0, 0)),
                pl.BlockSpec((1, rows, D), lambda j: (0, 0, 0)),
                pl.BlockSpec((D, tn), lambda j: (0, j)),
                pl.BlockSpec((D, tn), lambda j: (0, j + nj)),
                pl.BlockSpec((CONV_W, tn), lambda j: (0, j)),
                pl.BlockSpec((CONV_W, tn), lambda j: (0, j + nj)),
                pl.BlockSpec((1, tn), lambda j: (0, j)),
                pl.BlockSpec((1, tn), lambda j: (0, j + nj)),
                pl.BlockSpec((nb, CONV_W - 1, tn), lambda j: (0, 0, j)),
                pl.BlockSpec((nb, CONV_W - 1, tn), lambda j: (0, 0, j + nj))]
    z, ua, ub = pl.pallas_call(
        functools.partial(_upconv_step_kernel, rows=rows, seq=seq),
        grid=(nj,),
        in_specs=in_specs,
        out_specs=[pl.BlockSpec((rows, tn), lambda j: (0, j)),
                   pl.BlockSpec((rows, tn), lambda j: (0, j)),
                   pl.BlockSpec((rows, tn), lambda j: (0, j))],
        out_shape=[jax.ShapeDtypeStruct((rows, F), BF16), jax.ShapeDtypeStruct((rows, F), F32),
                   jax.ShapeDtypeStruct((rows, F), F32)],
        scratch_shapes=[pltpu.VMEM((rows, D), BF16)],
        compiler_params=_params(("arbitrary",)),
        name="upconv_step",
    )(x, g.reshape(1, D), sh, sc, w_up, w_up, conv_w, conv_w, conv_b.reshape(1, -1), conv_b.reshape(1, -1),
      state, state)
    return z, ua, ub


def _out_kernel(a_ref, w_ref, x_ref, gt_ref, o_ref):
    o_ref[0] = x_ref[0] + gt_ref[0] * _bdot(a_ref[0], w_ref[...])


def out_proj(a, w, x, gate, *, tm, tn):
    B, T, K = a.shape
    N = w.shape[1]
    R = gate.shape[1]
    if R == 1:
        gate_spec = pl.BlockSpec((1, 1, tn), lambda b, i, j: (b, 0, j))
    else:
        gate_spec = pl.BlockSpec((1, tm, tn), lambda b, i, j: (b, i, j))
    return pl.pallas_call(
        _out_kernel,
        grid=(B, T // tm, N // tn),
        in_specs=[pl.BlockSpec((1, tm, K), lambda b, i, j: (b, i, 0)),
                  pl.BlockSpec((K, tn), lambda b, i, j: (0, j)),
                  pl.BlockSpec((1, tm, tn), lambda b, i, j: (b, i, j)),
                  gate_spec],
        out_specs=pl.BlockSpec((1, tm, tn), lambda b, i, j: (b, i, j)),
        out_shape=jax.ShapeDtypeStruct((B, T, N), F32),
        compiler_params=_params(("arbitrary", "arbitrary", "arbitrary")),
        name="out_proj",
    )(a, w, x, gate)


def _rope_full(x, cos, sin):
    x1 = x[:, :LANES]
    x2 = x[:, LANES:]
    return jnp.concatenate([x1 * cos - x2 * sin, x2 * cos + x1 * sin], axis=1)


def _retention_head(q, k, v, gate, S, cos, sin, dec, qd, kd, gl, gn):
    q = _rope_full(q, cos, sin)
    k = _rope_full(k, cos, sin) * (RET_DK ** -0.5)
    qb = q.astype(BF16)
    vb = v.astype(BF16)
    scores = _bdot_nt(qb, k) * dec
    o = _bdot(scores, vb) + _bdot(qb, S) * qd
    kdt = (k * kd).T
    s_new = gl * S + _bdot(kdt, vb)
    mu = jnp.mean(o, axis=-1, keepdims=True)
    d = o - mu
    var = jnp.mean(d * d, axis=-1, keepdims=True)
    on = d * lax.rsqrt(var + EPS) * gn
    return _silu(gate) * on, s_new


def _ret_kernel(q_ref, k_ref, v_ref, g_ref, cos_ref, sin_ref, dec_ref, qd_ref, kd_ref, gl_ref, gn_ref,
                o_ref, s_ref):
    @pl.when(pl.program_id(2) == 0)
    def _():
        s_ref[...] = jnp.zeros_like(s_ref)

    o, s_new = _retention_head(q_ref[0], k_ref[0], v_ref[0], g_ref[0], s_ref[0, 0], cos_ref[...], sin_ref[...],
                               dec_ref[0], qd_ref[0], kd_ref[0], gl_ref[0], gn_ref[...])
    s_ref[0, 0] = s_new
    o_ref[0] = o.astype(o_ref.dtype)


def _decay_tables(L, Lp):
    log_gamma = jnp.log(1.0 - jnp.exp2(-5.0 - jnp.arange(RET_HEADS, dtype=F32)))
    idx = jnp.arange(L, dtype=F32)
    rel = idx[:, None] - idx[None, :]
    dec = jnp.where(rel >= 0, jnp.exp(jnp.maximum(rel, 0.0)[None] * log_gamma[:, None, None]), 0.0)
    qd = jnp.exp((idx + 1.0)[None, :] * log_gamma[:, None])[..., None]
    kd = jnp.exp((L - 1.0 - idx)[None, :] * log_gamma[:, None])[..., None]
    gl = jnp.exp(L * log_gamma).reshape(RET_HEADS, 1, 1)
    p = Lp - L
    return (jnp.pad(dec, ((0, 0), (0, p), (0, p))), jnp.pad(qd, ((0, 0), (0, p), (0, 0))),
            jnp.pad(kd, ((0, 0), (0, p), (0, 0))), gl)


def _rope_tables_full(pos):
    half = RET_DK // 2
    inv = RET_ROPE_BASE ** (-jnp.arange(half, dtype=F32) * 2.0 / RET_DK)
    ang = pos.astype(F32)[:, None] * inv[None, :]
    return jnp.cos(ang), jnp.sin(ang)


def retention_prompt(qkvg, gn, pos):
    B, T, _ = qkvg.shape
    L = RET_BLOCK
    H = RET_HEADS
    cos, sin = _rope_tables_full(pos)
    dec, qd, kd, gl = _decay_tables(L, L)
    kq = RET_DV // RET_DK
    in_specs = [pl.BlockSpec((1, L, RET_DK), lambda b, h, c: (b, c, h)),
                pl.BlockSpec((1, L, RET_DK), lambda b, h, c: (b, c, H + h)),
                pl.BlockSpec((1, L, RET_DV), lambda b, h, c: (b, c, 2 * H // kq + h)),
                pl.BlockSpec((1, L, RET_DV), lambda b, h, c: (b, c, 2 * H // kq + H + h)),
                pl.BlockSpec((L, LANES), lambda b, h, c: (c, 0)),
                pl.BlockSpec((L, LANES), lambda b, h, c: (c, 0)),
                pl.BlockSpec((1, L, L), lambda b, h, c: (h, 0, 0)),
                pl.BlockSpec((1, L, 1), lambda b, h, c: (h, 0, 0)),
                pl.BlockSpec((1, L, 1), lambda b, h, c: (h, 0, 0)),
                pl.BlockSpec((1, 1, 1), lambda b, h, c: (h, 0, 0)),
                pl.BlockSpec((1, RET_DV), lambda b, h, c: (0, h))]
    return pl.pallas_call(
        _ret_kernel,
        grid=(B, H, T // L),
        in_specs=in_specs,
        out_specs=[pl.BlockSpec((1, L, RET_DV), lambda b, h, c: (b, c, h)),
                   pl.BlockSpec((1, 1, RET_DK, RET_DV), lambda b, h, c: (b, h, 0, 0))],
        out_shape=[jax.ShapeDtypeStruct((B, T, H * RET_DV), BF16),
                   jax.ShapeDtypeStruct((B, H, RET_DK, RET_DV), F32)],
        compiler_params=_params(("arbitrary", "arbitrary", "arbitrary")),
        name="retention_prompt",
    )(qkvg, qkvg, qkvg, qkvg, cos, sin, dec, qd, kd, gl, gn.reshape(1, -1))


def _ret_step_kernel(x_ref, s_ref, cos_ref, sin_ref, dec_ref, qd_ref, kd_ref, gl_ref, gn_ref, o_ref, so_ref, *, seq):
    H = RET_HEADS
    zpad = jnp.zeros((RET_PAD - seq, RET_DV), F32)

    def padded(col0, width):
        return jnp.concatenate([x_ref[:, col0:col0 + width], zpad[:, :width]], axis=0)

    cos = cos_ref[...]
    sin = sin_ref[...]
    for h in range(H):
        q = padded(h * RET_DK, RET_DK)
        k = padded(H * RET_DK + h * RET_DK, RET_DK)
        v = padded(2 * H * RET_DK + h * RET_DV, RET_DV)
        gate = padded(2 * H * RET_DK + H * RET_DV + h * RET_DV, RET_DV)
        o, s_new = _retention_head(q, k, v, gate, s_ref[0, h], cos, sin, dec_ref[h], qd_ref[h], kd_ref[h],
                                   gl_ref[h], gn_ref[:, h * RET_DV:(h + 1) * RET_DV])
        so_ref[0, h] = s_new
        o_ref[:, h * RET_DV:(h + 1) * RET_DV] = o[:seq]


def retention_step(qkvg, state, gn, pos, *, seq):
    rows, width = qkvg.shape
    nb = rows // seq
    H = RET_HEADS
    cos, sin = _rope_tables_full(pos)
    pad = ((0, RET_PAD - seq), (0, 0))
    cos, sin = jnp.pad(cos, pad), jnp.pad(sin, pad)
    dec, qd, kd, gl = _decay_tables(seq, RET_PAD)
    whole = lambda shape: pl.BlockSpec(shape, lambda b: (0,) * len(shape))
    return pl.pallas_call(
        functools.partial(_ret_step_kernel, seq=seq),
        grid=(nb,),
        in_specs=[pl.BlockSpec((seq, width), lambda b: (b, 0)),
                  pl.BlockSpec((1, H, RET_DK, RET_DV), lambda b: (b, 0, 0, 0)),
                  whole((RET_PAD, LANES)), whole((RET_PAD, LANES)),
                  whole((H, RET_PAD, RET_PAD)), whole((H, RET_PAD, 1)), whole((H, RET_PAD, 1)), whole((H, 1, 1)),
                  whole((1, H * RET_DV))],
        out_specs=[pl.BlockSpec((seq, H * RET_DV), lambda b: (b, 0)),
                   pl.BlockSpec((1, H, RET_DK, RET_DV), lambda b: (b, 0, 0, 0))],
        out_shape=[jax.ShapeDtypeStruct((rows, H * RET_DV), F32),
                   jax.ShapeDtypeStruct((nb, H, RET_DK, RET_DV), F32)],
        compiler_params=_params(("arbitrary",)),
        name="retention_step",
    )(qkvg, state, cos, sin, dec, qd, kd, gl, gn.reshape(1, -1))


def _head_pairs(kcat, vcat):
    lo = lax.broadcasted_iota(jnp.int32, (kcat.shape[0], LANES), 1) < HEAD_DIM
    for pair in range(N_KV_HEADS // 2):
        kp = kcat[:, pair * LANES:(pair + 1) * LANES]
        vp = vcat[:, pair * LANES:(pair + 1) * LANES]
        kr = pltpu.roll(kp, HEAD_DIM, axis=1)
        vr = pltpu.roll(vp, HEAD_DIM, axis=1)
        for sub in range(2):
            ka, kb_ = (kp, kr) if sub == 0 else (kr, kp)
            va, vb_ = (vp, vr) if sub == 0 else (vr, vp)
            halves = [(jnp.where(lo, ka, 0.0).astype(BF16), jnp.where(lo, va, 0.0).astype(BF16)),
                      (jnp.where(lo, 0.0, kb_).astype(BF16), jnp.where(lo, 0.0, vb_).astype(BF16))]
            yield pair * 2 + sub, halves


def _sink_softmax(s, ok, sink):
    s = jnp.where(ok, s * (HEAD_DIM ** -0.5), NEG_INF)
    m = jnp.maximum(jnp.max(s, axis=-1, keepdims=True), sink)
    p = jnp.exp(s - m)
    return p * (1.0 / (jnp.sum(p, axis=-1, keepdims=True) + jnp.exp(sink - m)))


def _attend(q_of, kcat, vcat, ok, sink_ref, rows, store):
    pieces = GQA_GROUPS // 2
    for kh, halves in _head_pairs(kcat, vcat):
        base = kh * GQA_GROUPS * HEAD_DIM
        qs = jnp.concatenate([q_of(base + g * LANES) for g in range(pieces)], axis=0).astype(BF16)
        out = None
        for half, (kk, vv) in enumerate(halves):
            s = _bdot_nt(qs, kk)
            p = jnp.concatenate(
                [_sink_softmax(s[g * rows:(g + 1) * rows], ok, sink_ref[kh * GQA_GROUPS + 2 * g + half])
                 for g in range(pieces)], axis=0)
            o = _bdot(p, vv)
            out = o if out is None else out + o
        for g in range(pieces):
            store(base + g * LANES, out[g * rows:(g + 1) * rows])


def _swa_kernel(sink_ref, q_ref, kp_ref, kc_ref, vp_ref, vc_ref, o_ref):
    i = pl.program_id(1)
    kcat = jnp.concatenate([kp_ref[0], kc_ref[0]], axis=0)
    vcat = jnp.concatenate([vp_ref[0], vc_ref[0]], axis=0)
    ql = lax.broadcasted_iota(jnp.int32, (ATT_BLOCK, 2 * ATT_BLOCK), 0)
    km = lax.broadcasted_iota(jnp.int32, (ATT_BLOCK, 2 * ATT_BLOCK), 1)
    no_prev = jnp.where(i > 0, 0, 4 * ATT_BLOCK)
    ok = ((km < ATT_BLOCK) & (km >= ql + no_prev)) | ((km >= ATT_BLOCK) & (km - ATT_BLOCK <= ql))

    def store(col0, val):
        o_ref[0, :, col0:col0 + LANES] = val.astype(o_ref.dtype)

    _attend(lambda c0: q_ref[0, :, c0:c0 + LANES], kcat, vcat, ok, sink_ref, ATT_BLOCK, store)


def swa_prompt(q, k, v, sinks):
    B, T, DQ = q.shape
    DKV = k.shape[2]
    cur = lambda b, i: (b, i, 0)
    prev = lambda b, i: (b, jnp.maximum(i - 1, 0), 0)
    return pl.pallas_call(
        _swa_kernel,
        grid=(B, T // ATT_BLOCK),
        in_specs=[pl.BlockSpec(memory_space=pltpu.SMEM),
                  pl.BlockSpec((1, ATT_BLOCK, DQ), cur),
                  pl.BlockSpec((1, ATT_BLOCK, DKV), prev), pl.BlockSpec((1, ATT_BLOCK, DKV), cur),
                  pl.BlockSpec((1, ATT_BLOCK, DKV), prev), pl.BlockSpec((1, ATT_BLOCK, DKV), cur)],
        out_specs=pl.BlockSpec((1, ATT_BLOCK, DQ), cur),
        out_shape=jax.ShapeDtypeStruct((B, T, DQ), BF16),
        compiler_params=_params(("arbitrary", "arbitrary")),
        name="swa_prompt",
    )(sinks, q, k, k, v, v)


def _swa_step_kernel(sink_ref, q_ref, ck_ref, nk_ref, cv_ref, nv_ref, o_ref, *, seq, wb):
    zpad = jnp.zeros((2 * ATT_BLOCK - wb - seq, nk_ref.shape[1]), F32)
    kcat = jnp.concatenate([ck_ref[0], nk_ref[...], zpad], axis=0)
    vcat = jnp.concatenate([cv_ref[0], nv_ref[...], zpad], axis=0)
    nk = kcat.shape[0]
    t = lax.broadcasted_iota(jnp.int32, (seq, nk), 0)
    s = lax.broadcasted_iota(jnp.int32, (seq, nk), 1)
    rel = t + wb - s
    ok = (rel >= 0) & (rel <= WINDOW)

    def store(col0, val):
        o_ref[:, col0:col0 + LANES] = val

    _attend(lambda c0: q_ref[:, c0:c0 + LANES], kcat, vcat, ok, sink_ref, seq, store)


def swa_step(q, cache_k, cache_v, k_new, v_new, sinks, *, seq):
    rows, DQ = q.shape
    nb, wb, DKV = cache_k.shape
    return pl.pallas_call(
        functools.partial(_swa_step_kernel, seq=seq, wb=wb),
        grid=(nb,),
        in_specs=[pl.BlockSpec(memory_space=pltpu.SMEM),
                  pl.BlockSpec((seq, DQ), lambda b: (b, 0)),
                  pl.BlockSpec((1, wb, DKV), lambda b: (b, 0, 0)), pl.BlockSpec((seq, DKV), lambda b: (b, 0)),
                  pl.BlockSpec((1, wb, DKV), lambda b: (b, 0, 0)), pl.BlockSpec((seq, DKV), lambda b: (b, 0))],
        out_specs=pl.BlockSpec((seq, DQ), lambda b: (b, 0)),
        out_shape=jax.ShapeDtypeStruct((rows, DQ), F32),
        compiler_params=_params(("arbitrary",)),
        name="swa_step",
    )(sinks, q, cache_k, k_new, cache_v, v_new)


def _rope_tables_partial(pos):
    half = ROPE_DIM // 2
    inv = ROPE_THETA ** (-jnp.arange(half, dtype=F32) * 2.0 / ROPE_DIM)
    ang = pos.astype(F32)[:, None] * inv[None, :]
    cos, sin = jnp.cos(ang), jnp.sin(ang)
    T = pos.shape[0]
    ones = jnp.ones((T, HEAD_DIM - ROPE_DIM), F32)
    zeros = jnp.zeros((T, HEAD_DIM - ROPE_DIM), F32)
    zh = jnp.zeros((T, half), F32)
    c = jnp.concatenate([cos, cos, ones], axis=1)
    s1 = jnp.concatenate([-sin, zh, zeros], axis=1)
    s2 = jnp.concatenate([zh, sin, zeros], axis=1)
    rep = LANES // HEAD_DIM
    return tuple(jnp.tile(a, (1, rep)) for a in (c, s1, s2))


def _final_norm_kernel(x_ref, g_ref, o_ref):
    x = x_ref[0]
    ms = jnp.mean(x * x, axis=-1, keepdims=True)
    o_ref[0] = x * lax.rsqrt(ms + EPS) * g_ref[...]


def final_norm(x, g, tm):
    B, T, D = x.shape
    return pl.pallas_call(
        _final_norm_kernel,
        grid=(B, T // tm),
        in_specs=[pl.BlockSpec((1, tm, D), lambda b, i: (b, i, 0)), pl.BlockSpec((1, D), lambda b, i: (0, 0))],
        out_specs=pl.BlockSpec((1, tm, D), lambda b, i: (b, i, 0)),
        out_shape=jax.ShapeDtypeStruct((B, T, D), F32),
        compiler_params=_params(("arbitrary", "arbitrary")),
        name="final_norm",
    )(x, g.reshape(1, D))


def kernel(x_prompt, x_sample, state_ret, cache_win_k, cache_win_v, state_conv, c_prompt, c_sample,
           w_ada, b_ada, norm_mix, norm_ffn, ret_w_in, ret_gn, ret_w_out,
           kv_norm, kv_w_ada, kv_b_ada, w_kv, att_w_q, att_sinks, att_w_o,
           ffn_w_up, ffn_conv_w, ffn_conv_b, ffn_w_down, norm_f):
    D = D_MODEL
    BP, TP, _ = x_prompt.shape
    BS, TS, _ = x_sample.shape
    RS = BS * TS
    KV = N_KV_HEADS * HEAD_DIM

    c_all = jnp.concatenate([c_prompt, c_sample], axis=0)
    c_all = jnp.pad(c_all, ((0, -c_all.shape[0] % 16), (0, 0)))
    mods = ada_mods(c_all, w_ada, b_ada)
    kv_mods = ada_mods(c_all, kv_w_ada[None], kv_b_ada[None])[0]

    def prompt_mods(m, n):
        m = m[:BP].reshape(BP, n, 1, D)
        return [m[:, k] for k in range(n)]

    def sample_mods(m, n):
        m = jnp.repeat(m[BP:BP + BS], TS, axis=0).reshape(1, RS, n, D)
        return [m[:, :, k] for k in range(n)]

    pos_p = jnp.arange(TP, dtype=jnp.int32)
    pos_s = PAST_LEN + jnp.arange(TS, dtype=jnp.int32)
    tabs_p = _rope_tables_partial(pos_p)
    tabs_s = tuple(jnp.tile(a, (BS, 1)) for a in _rope_tables_partial(pos_s))

    x = x_prompt
    ret_p, conv_p = [], []
    for l in range(DEPTH):
        if l == N_A:
            kv_sh, kv_sc = prompt_mods(kv_mods, 2)
            kv_p = proj(x, kv_norm, kv_sh, kv_sc, w_kv, tm=1024, tn=2 * KV, out_dtype=F32,
                        rope_tabs=tabs_p, rope_cols=KV)
            k_p, v_p = kv_p[..., :KV], kv_p[..., KV:]
        sh1, sc1, g1, sh2, sc2, g2 = prompt_mods(mods[l], N_MOD)
        if l < N_A:
            qkvg = proj(x, norm_mix[l], sh1, sc1, ret_w_in[l], tm=1024, tn=512, out_dtype=F32)
            a, s_new = retention_prompt(qkvg, ret_gn[l], pos_p)
            ret_p.append(s_new)
            x = out_proj(a, ret_w_out[l], x, g1, tm=1024, tn=256)
        else:
            j = l - N_A
            q = proj(x, norm_mix[l], sh1, sc1, att_w_q[j], tm=1024, tn=512, out_dtype=BF16,
                     rope_tabs=tabs_p, rope_cols=D)
            a = swa_prompt(q, k_p, v_p, att_sinks[j])
            x = out_proj(a, att_w_o[j], x, g1, tm=1024, tn=512)
        z, st = upconv_prompt(x, norm_ffn[l], sh2, sc2, ffn_w_up[l], ffn_conv_w[l], ffn_conv_b[l])
        conv_p.append(st[:, -1, :, 8 - (CONV_W - 1):, :].transpose(0, 2, 1, 3).reshape(BP, CONV_W - 1, 2 * D_FF))
        x = out_proj(z, ffn_w_down[l], x, g2, tm=1024, tn=256)
    y_prompt = final_norm(x, norm_f, 512)
    wp = min(WINDOW, TP)
    win_k_prompt = k_p[:, -wp:].reshape(BP, wp, N_KV_HEADS, HEAD_DIM)
    win_v_prompt = v_p[:, -wp:].reshape(BP, wp, N_KV_HEADS, HEAD_DIM)

    x = x_sample.reshape(1, RS, D)
    wb = cache_win_k.shape[1]
    ck = cache_win_k.reshape(BS, wb, KV)
    cv = cache_win_v.reshape(BS, wb, KV)
    ret_s, conv_s = [], []
    for l in range(DEPTH):
        if l == N_A:
            kv_sh, kv_sc = sample_mods(kv_mods, 2)
            kv_s = proj(x, kv_norm, kv_sh, kv_sc, w_kv, tm=RS, tn=2 * KV, out_dtype=F32,
                        rope_tabs=tabs_s, rope_cols=KV)[0]
            k_s, v_s = kv_s[:, :KV], kv_s[:, KV:]
        sh1, sc1, g1, sh2, sc2, g2 = sample_mods(mods[l], N_MOD)
        if l < N_A:
            qkvg = proj(x, norm_mix[l], sh1, sc1, ret_w_in[l], tm=RS, tn=1024, out_dtype=F32)[0]
            a, s_new = retention_step(qkvg, state_ret[l], ret_gn[l], pos_s, seq=TS)
            ret_s.append(s_new)
            x = out_proj(a[None], ret_w_out[l], x, g1, tm=RS, tn=512)
        else:
            j = l - N_A
            q = proj(x, norm_mix[l], sh1, sc1, att_w_q[j], tm=RS, tn=1024, out_dtype=F32,
                     rope_tabs=tabs_s, rope_cols=D)[0]
            a = swa_step(q, ck, cv, k_s, v_s, att_sinks[j], seq=TS)
            x = out_proj(a[None], att_w_o[j], x, g1, tm=RS, tn=1024)
        z, ua, ub = upconv_step(x, norm_ffn[l], sh2, sc2, ffn_w_up[l], ffn_conv_w[l], ffn_conv_b[l],
                                state_conv[l], seq=TS)
        u_last = jnp.concatenate([ua.reshape(BS, TS, D_FF)[:, TS - (CONV_W - 1):],
                                  ub.reshape(BS, TS, D_FF)[:, TS - (CONV_W - 1):]], axis=-1)
        conv_s.append(u_last)
        x = out_proj(z[None], ffn_w_down[l], x, g2, tm=RS, tn=512)
    y_sample = final_norm(x, norm_f, RS).reshape(BS, TS, D)
    k_all = jnp.concatenate([ck, k_s.reshape(BS, TS, KV)], axis=1)
    v_all = jnp.concatenate([cv, v_s.reshape(BS, TS, KV)], axis=1)
    win_k_sample = k_all[:, -wb:].reshape(BS, wb, N_KV_HEADS, HEAD_DIM)
    win_v_sample = v_all[:, -wb:].reshape(BS, wb, N_KV_HEADS, HEAD_DIM)

    return (y_prompt, y_sample, jnp.stack(ret_p), jnp.stack(ret_s), win_k_prompt, win_v_prompt,
            win_k_sample, win_v_sample, jnp.stack(conv_p), jnp.stack(conv_s))
```

```python
import functools

import jax
import jax.numpy as jnp
from jax import lax
from jax.experimental import pallas as pl
from jax.experimental.pallas import tpu as pltpu

F32 = jnp.float32
BF16 = jnp.bfloat16

D_MODEL = 2048
DEPTH = 4
PAST_LEN = 16384
N_A = DEPTH // 2
RET_HEADS = 8
RET_DK = D_MODEL // RET_HEADS
RET_DV = 2 * D_MODEL // RET_HEADS
RET_ROPE_BASE = 10000.0
N_Q_HEADS = 32
N_KV_HEADS = 4
HEAD_DIM = D_MODEL // N_Q_HEADS
GQA_GROUPS = N_Q_HEADS // N_KV_HEADS
ROPE_DIM = HEAD_DIM // 4
ROPE_THETA = 500000.0
WINDOW = 128
ATT_BLOCK = 128
D_FF = 2 * D_MODEL
CONV_W = 3
N_MOD = 6
EPS = 1e-6
NEG_INF = -1e30

LANES = 128
SUBLANES = 8
PACK = 16
HALO = PACK
RET_BLOCK = 256
RET_HEADS_PER_STEP = 2
RET_PAD = PACK
VMEM_LIMIT = 56 * 1024 * 1024


def _params(sem):
    return pltpu.CompilerParams(dimension_semantics=sem, vmem_limit_bytes=VMEM_LIMIT)


def _silu(x):
    return x * jax.nn.sigmoid(x)


def _bdot(a, b):
    return jnp.dot(a.astype(BF16), b.astype(BF16), preferred_element_type=F32)


def _bdot_nt(a, b):
    return lax.dot_general(a.astype(BF16), b.astype(BF16), (((1,), (1,)), ((), ())), preferred_element_type=F32)


def _norm_mod(x, g, sc, sh):
    ms = jnp.mean(x * x, axis=-1, keepdims=True)
    y = x * lax.rsqrt(ms + EPS) * g
    return y * (1.0 + sc) + sh


class Mods:
    def __init__(self, arr, n_sample):
        self.a3 = arr
        self.a4 = arr.reshape(arr.shape[0], arr.shape[1], 1, arr.shape[2])
        self.n_sample = n_sample

    def prompt(self, l, k, width, col=lambda *g: 0):
        per = D_MODEL // width
        off = self.n_sample
        return self.a4, pl.BlockSpec((1, 1, 1, width), lambda *g: (l, off + g[0], 0, k * per + col(*g)))

    def sample(self, l, k, width, col=lambda *g: 0):
        per = D_MODEL // width
        return self.a3, pl.BlockSpec((1, self.n_sample, width), lambda *g: (l, 0, k * per + col(*g)))


def _ada_kernel(c_ref, w_ref, b_ref, o_ref):
    o_ref[0] = _bdot(_silu(c_ref[...]), w_ref[0]) + b_ref[0]


def ada_mods(c_all, w, b, tn=1024):
    L, D, N = w.shape
    R = c_all.shape[0]
    return pl.pallas_call(
        _ada_kernel,
        grid=(L, N // tn),
        in_specs=[pl.BlockSpec((R, D), lambda l, j: (0, 0)),
                  pl.BlockSpec((1, D, tn), lambda l, j: (l, 0, j)),
                  pl.BlockSpec((1, 1, tn), lambda l, j: (l, 0, j))],
        out_specs=pl.BlockSpec((1, R, tn), lambda l, j: (l, 0, j)),
        out_shape=jax.ShapeDtypeStruct((L, R, N), F32),
        compiler_params=_params(("arbitrary", "arbitrary")),
        name="ada_mods",
    )(c_all, w, b.reshape(L, 1, N))


def _store_normed(h_scr, r_scr, row0, x_ref, g_ref, sc_ref, sh_ref, rows, tok):
    def stats(r, carry):
        r0 = pl.multiple_of(r * PACK, PACK)
        x = x_ref[0, pl.ds(r0, PACK), :]
        r_scr[pl.ds(r0, PACK), :] = lax.rsqrt(jnp.mean(x * x, axis=-1, keepdims=True) + EPS)
        return carry

    lax.fori_loop(0, rows // PACK, stats, 0, unroll=4)
    g = g_ref[0]

    def piece(start, n, sc, sh):
        return x_ref[0, pl.ds(start, n), :] * r_scr[pl.ds(start, n), :] * g * (1.0 + sc) + sh

    def apply(r, carry):
        r0 = pl.multiple_of(r * PACK, PACK)
        if tok is None:
            h = piece(r0, PACK, sc_ref[0, 0], sh_ref[0, 0])
        else:
            parts = []
            for s in range(PACK // tok):
                m = r * (PACK // tok) + s
                start = pl.multiple_of(r0 + s * tok, tok)
                parts.append(piece(start, tok, sc_ref[0, pl.ds(m, 1), :], sh_ref[0, pl.ds(m, 1), :]))
            h = jnp.concatenate(parts, axis=0)
        h_scr[pl.ds(pl.multiple_of(row0 + r0, PACK), PACK), :] = h.astype(BF16)
        return carry

    lax.fori_loop(0, rows // PACK, apply, 0, unroll=2)


def _rope64(a, c, s1, s2):
    return a * c + pltpu.roll(a, LANES - ROPE_DIM // 2, axis=1) * s1 + pltpu.roll(a, ROPE_DIM // 2, axis=1) * s2


def _proj_kernel(*refs, tm, tn, tok, rope, rope_cols, scale_from):
    if rope:
        x_ref, g_ref, sh_ref, sc_ref, w_ref, t1_ref, t2_ref, t3_ref, o_ref, h_scr, r_scr = refs
    else:
        x_ref, g_ref, sh_ref, sc_ref, w_ref, o_ref, h_scr, r_scr = refs
    j = pl.program_id(2)

    @pl.when(j == 0)
    def _():
        _store_normed(h_scr, r_scr, 0, x_ref, g_ref, sc_ref, sh_ref, tm, tok)

    acc = _bdot(h_scr[...], w_ref[0])

    def plain():
        o_ref[0] = acc.astype(o_ref.dtype)

    if rope is None:
        plain()
    elif rope == "partial":
        for c in range(tn // LANES):
            a = acc[:, c * LANES:(c + 1) * LANES]
            if c * LANES < rope_cols:
                a = _rope64(a, t1_ref[...], t2_ref[...], t3_ref[...])
            o_ref[0, :, c * LANES:(c + 1) * LANES] = a.astype(o_ref.dtype)
    else:
        @pl.when(j * tn < rope_cols)
        def _():
            cos, sin = t1_ref[...], t2_ref[...]
            scale = jnp.where(j * tn >= scale_from, RET_DK ** -0.5, 1.0).astype(F32)
            for hd in range(tn // RET_DK):
                x1 = acc[:, hd * RET_DK:hd * RET_DK + LANES]
                x2 = acc[:, hd * RET_DK + LANES:(hd + 1) * RET_DK]
                o_ref[0, :, hd * RET_DK:hd * RET_DK + LANES] = ((x1 * cos - x2 * sin) * scale).astype(o_ref.dtype)
                o_ref[0, :, hd * RET_DK + LANES:(hd + 1) * RET_DK] = ((x2 * cos + x1 * sin) * scale).astype(o_ref.dtype)

        pl.when(j * tn >= rope_cols)(plain)


def proj(x, g, l, sh, sc, w, wl, *, tm, tn, tok, out_dtype, rope=None, rope_tabs=None, rope_cols=0, scale_from=0):
    B, T, D = x.shape
    N = w.shape[2]
    assert rope != "partial" or rope_cols == N or tn == N
    in_specs = [pl.BlockSpec((1, tm, D), lambda b, i, j: (b, i, 0)),
                pl.BlockSpec((1, 1, D), lambda b, i, j: (l, 0, 0)),
                sh[1], sc[1],
                pl.BlockSpec((1, D, tn), lambda b, i, j: (wl, 0, j))]
    args = [x, g.reshape(g.shape[0], 1, D), sh[0], sc[0], w]
    if rope:
        tabs = list(rope_tabs) + ([rope_tabs[0]] if len(rope_tabs) == 2 else [])
        in_specs += [pl.BlockSpec((tm, LANES), lambda b, i, j: (i, 0))] * 3
        args += tabs
    return pl.pallas_call(
        functools.partial(_proj_kernel, tm=tm, tn=tn, tok=tok, rope=rope, rope_cols=rope_cols, scale_from=scale_from),
        grid=(B, T // tm, N // tn),
        in_specs=in_specs,
        out_specs=pl.BlockSpec((1, tm, tn), lambda b, i, j: (b, i, j)),
        out_shape=jax.ShapeDtypeStruct((B, T, N), out_dtype),
        scratch_shapes=[pltpu.VMEM((tm, D), BF16), pltpu.VMEM((tm, 1), F32)],
        compiler_params=_params(("arbitrary", "arbitrary", "arbitrary")),
        name="proj",
    )(*args)


def _conv_gate(ua, ub, cwa_ref, cwb_ref, cba_ref, cbb_ref):
    def conv(u, cw_ref, cb_ref):
        acc = cb_ref[0] + cw_ref[0, 0:1, :] * u[0]
        acc = acc + cw_ref[0, 1:2, :] * u[1]
        return acc + cw_ref[0, 2:3, :] * u[2]

    return _silu(conv(ua, cwa_ref, cba_ref)) * conv(ub, cwb_ref, cbb_ref)


def _upconv_kernel(x_ref, xh_ref, g_ref, sh_ref, sc_ref, wa_ref, wb_ref, cwa_ref, cwb_ref, cba_ref, cbb_ref,
                   z_ref, st_ref, h_scr, r_scr, *, tm):
    i = pl.program_id(1)

    @pl.when(pl.program_id(2) == 0)
    def _():
        _store_normed(h_scr, r_scr, HALO, x_ref, g_ref, sc_ref, sh_ref, tm, None)
        hh = _norm_mod(xh_ref[0], g_ref[0], sc_ref[0, 0], sh_ref[0, 0])
        h_scr[0:HALO, :] = jnp.where(i > 0, hh, 0.0).astype(BF16)

    h = h_scr[...]

    def taps(w_ref):
        u = _bdot(h, w_ref[0])
        return (pltpu.roll(u, 2, axis=0)[HALO:], pltpu.roll(u, 1, axis=0)[HALO:], u[HALO:])

    ua = taps(wa_ref)
    ub = taps(wb_ref)
    z_ref[0] = _conv_gate(ua, ub, cwa_ref, cwb_ref, cba_ref, cbb_ref).astype(z_ref.dtype)
    st_ref[0, 0, 0] = ua[2][tm - SUBLANES:, :]
    st_ref[0, 0, 1] = ub[2][tm - SUBLANES:, :]


def _upconv_weight_specs(l, nj, tn, D, col):
    return [pl.BlockSpec((1, D, tn), lambda *g: (l, 0, col(*g))),
            pl.BlockSpec((1, D, tn), lambda *g: (l, 0, col(*g) + nj)),
            pl.BlockSpec((1, CONV_W, tn), lambda *g: (l, 0, col(*g))),
            pl.BlockSpec((1, CONV_W, tn), lambda *g: (l, 0, col(*g) + nj)),
            pl.BlockSpec((1, 1, tn), lambda *g: (l, 0, col(*g))),
            pl.BlockSpec((1, 1, tn), lambda *g: (l, 0, col(*g) + nj))]


def upconv_prompt(x, g, l, sh, sc, w_up, conv_w, conv_b, *, tm=1024, tn=512):
    B, T, D = x.shape
    F = w_up.shape[2] // 2
    nj = F // tn
    L = w_up.shape[0]
    in_specs = [pl.BlockSpec((1, tm, D), lambda b, i, j: (b, i, 0)),
                pl.BlockSpec((1, HALO, D), lambda b, i, j: (b, jnp.maximum(i * (tm // HALO) - 1, 0), 0)),
                pl.BlockSpec((1, 1, D), lambda b, i, j: (l, 0, 0)),
                sh[1], sc[1]] + _upconv_weight_specs(l, nj, tn, D, lambda b, i, j: j)
    cb = conv_b.reshape(L, 1, 2 * F)
    return pl.pallas_call(
        functools.partial(_upconv_kernel, tm=tm),
        grid=(B, T // tm, nj),
        in_specs=in_specs,
        out_specs=[pl.BlockSpec((1, tm, tn), lambda b, i, j: (b, i, j)),
                   pl.BlockSpec((1, 1, 2, SUBLANES, tn), lambda b, i, j: (b, i, 0, 0, j))],
        out_shape=[jax.ShapeDtypeStruct((B, T, F), BF16),
                   jax.ShapeDtypeStruct((B, T // tm, 2, SUBLANES, F), F32)],
        scratch_shapes=[pltpu.VMEM((HALO + tm, D), BF16), pltpu.VMEM((tm, 1), F32)],
        compiler_params=_params(("arbitrary", "arbitrary", "arbitrary")),
        name="upconv_prompt",
    )(x, x, g.reshape(L, 1, D), sh[0], sc[0], w_up, w_up, conv_w, conv_w, cb, cb)


def _upconv_step_kernel(x_ref, g_ref, sh_ref, sc_ref, wa_ref, wb_ref, cwa_ref, cwb_ref, cba_ref, cbb_ref,
                        sta_ref, stb_ref, z_ref, ua_ref, ub_ref, h_scr, r_scr, *, rows, seq):
    @pl.when(pl.program_id(0) == 0)
    def _():
        _store_normed(h_scr, r_scr, 0, x_ref, g_ref, sc_ref, sh_ref, rows, seq)

    h = h_scr[...]
    nb = rows // seq

    def taps(w_ref, st_ref, u_ref):
        u = _bdot(h, w_ref[0])
        u_ref[...] = u
        u3 = u.reshape(nb, seq, u.shape[1])
        t = lax.broadcasted_iota(jnp.int32, u3.shape, 1)
        st0 = st_ref[0, :, 0:1, :]
        st1 = st_ref[0, :, 1:2, :]
        um1 = jnp.where(t == 0, st1, pltpu.roll(u3, 1, axis=1))
        um2 = jnp.where(t == 0, st0, jnp.where(t == 1, st1, pltpu.roll(u3, 2, axis=1)))
        return (um2, um1, u3)

    ua = taps(wa_ref, sta_ref, ua_ref)
    ub = taps(wb_ref, stb_ref, ub_ref)
    z = _conv_gate(ua, ub, cwa_ref, cwb_ref, cba_ref, cbb_ref)
    z_ref[...] = z.reshape(rows, z.shape[2]).astype(z_ref.dtype)


def upconv_step(x, g, l, sh, sc, w_up, conv_w, conv_b, state, *, seq, tn=512):
    _, rows, D = x.shape
    F = w_up.shape[2] // 2
    nj = F // tn
    nb = rows // seq
    L = w_up.shape[0]
    in_specs = [pl.BlockSpec((1, rows, D), lambda j: (0, 0, 0)),
                pl.BlockSpec((1, 1, D), lambda j: (l, 0, 0)),
                sh[1], sc[1]] + _upconv_weight_specs(l, nj, tn, D, lambda j: j) + [
                pl.BlockSpec((1, nb, CONV_W - 1, tn), lambda j: (l, 0, 0, j)),
                pl.BlockSpec((1, nb, CONV_W - 1, tn), lambda j: (l, 0, 0, j + nj))]
    cb = conv_b.reshape(L, 1, 2 * F)
    return pl.pallas_call(
        functools.partial(_upconv_step_kernel, rows=rows, seq=seq),
        grid=(nj,),
        in_specs=in_specs,
        out_specs=[pl.BlockSpec((rows, tn), lambda j: (0, j))] * 3,
        out_shape=[jax.ShapeDtypeStruct((rows, F), BF16), jax.ShapeDtypeStruct((rows, F), F32),
                   jax.ShapeDtypeStruct((rows, F), F32)],
        scratch_shapes=[pltpu.VMEM((rows, D), BF16), pltpu.VMEM((rows, 1), F32)],
        compiler_params=_params(("arbitrary",)),
        name="upconv_step",
    )(x, g.reshape(L, 1, D), sh[0], sc[0], w_up, w_up, conv_w, conv_w, cb, cb, state, state)


def _out_kernel(a_ref, w_ref, x_ref, gt_ref, o_ref, *, tok):
    y = _bdot(a_ref[0], w_ref[0])
    if tok is None:
        o_ref[0] = x_ref[0] + gt_ref[0, 0] * y
    else:
        for b in range(y.shape[0] // tok):
            rows = slice(b * tok, (b + 1) * tok)
            o_ref[0, rows, :] = x_ref[0, rows, :] + gt_ref[0, b:b + 1, :] * y[rows]


def out_proj(a, w, l, x, gate, *, tm, tn, tok):
    B, T, K = a.shape
    N = w.shape[2]
    return pl.pallas_call(
        functools.partial(_out_kernel, tok=tok),
        grid=(B, T // tm, N // tn),
        in_specs=[pl.BlockSpec((1, tm, K), lambda b, i, j: (b, i, 0)),
                  pl.BlockSpec((1, K, tn), lambda b, i, j: (l, 0, j)),
                  pl.BlockSpec((1, tm, tn), lambda b, i, j: (b, i, j)),
                  gate[1]],
        out_specs=pl.BlockSpec((1, tm, tn), lambda b, i, j: (b, i, j)),
        out_shape=jax.ShapeDtypeStruct((B, T, N), F32),
        compiler_params=_params(("arbitrary", "arbitrary", "arbitrary")),
        name="out_proj",
    )(a, w, x, gate[0])


def _rope_full(x, cos, sin):
    x1 = x[:, :LANES]
    x2 = x[:, LANES:]
    return jnp.concatenate([x1 * cos - x2 * sin, x2 * cos + x1 * sin], axis=1)


def _retention_head(q, k, v, gate, S, dec, qd, kd, gl, gn):
    qb = q.astype(BF16)
    vb = v.astype(BF16)
    scores = _bdot_nt(qb, k) * dec
    o = _bdot(scores, vb) + _bdot(qb, S) * qd
    kdt = (k * kd).T
    s_new = gl * S + _bdot(kdt, vb)
    mu = jnp.mean(o, axis=-1, keepdims=True)
    d = o - mu
    var = jnp.mean(d * d, axis=-1, keepdims=True)
    on = d * lax.rsqrt(var + EPS) * gn
    return _silu(gate) * on, s_new


def _ret_kernel(q_ref, k_ref, v_ref, g_ref, dec_ref, qd_ref, kd_ref, gl_ref, gn_ref, *rest):
    o_ref, s_ref = rest[-2:]

    @pl.when(pl.program_id(2) == 0)
    def _():
        s_ref[...] = jnp.zeros_like(s_ref)

    for hd in range(RET_HEADS_PER_STEP):
        kc = slice(hd * RET_DK, (hd + 1) * RET_DK)
        vc = slice(hd * RET_DV, (hd + 1) * RET_DV)
        o, s_new = _retention_head(q_ref[0, :, kc], k_ref[0, :, kc], v_ref[0, :, vc], g_ref[0, :, vc],
                                   s_ref[0, 0, hd], dec_ref[hd], qd_ref[hd], kd_ref[hd], gl_ref[hd], gn_ref[0, :, vc])
        s_ref[0, 0, hd] = s_new
        o_ref[0, :, vc] = o.astype(o_ref.dtype)


def _decay_tables(L, Lp):
    log_gamma = jnp.log(1.0 - jnp.exp2(-5.0 - jnp.arange(RET_HEADS, dtype=F32)))
    idx = jnp.arange(L, dtype=F32)
    rel = idx[:, None] - idx[None, :]
    dec = jnp.where(rel >= 0, jnp.exp(jnp.maximum(rel, 0.0)[None] * log_gamma[:, None, None]), 0.0)
    qd = jnp.exp((idx + 1.0)[None, :] * log_gamma[:, None])[..., None]
    kd = jnp.exp((L - 1.0 - idx)[None, :] * log_gamma[:, None])[..., None]
    gl = jnp.exp(L * log_gamma).reshape(RET_HEADS, 1, 1)
    p = Lp - L
    return (jnp.pad(dec, ((0, 0), (0, p), (0, p))), jnp.pad(qd, ((0, 0), (0, p), (0, 0))),
            jnp.pad(kd, ((0, 0), (0, p), (0, 0))), gl)


def _rope_tables_full(pos):
    half = RET_DK // 2
    inv = RET_ROPE_BASE ** (-jnp.arange(half, dtype=F32) * 2.0 / RET_DK)
    ang = pos.astype(F32)[:, None] * inv[None, :]
    return jnp.cos(ang), jnp.sin(ang)


def _chain(prev, in_specs, args, out_index):
    if prev is None:
        return {}
    in_specs.append(pl.BlockSpec(memory_space=pl.ANY))
    args.append(prev)
    return {len(args) - 1: out_index}


def retention_prompt(qkvg, gn, l, states):
    B, T, _ = qkvg.shape
    L = RET_BLOCK
    H = RET_HEADS
    P = RET_HEADS_PER_STEP
    dec, qd, kd, gl = _decay_tables(L, L)
    qk_blocks = H // P
    v0 = 2 * H * RET_DK // (P * RET_DV)
    in_specs = [pl.BlockSpec((1, L, P * RET_DK), lambda b, h, c: (b, c, h)),
                pl.BlockSpec((1, L, P * RET_DK), lambda b, h, c: (b, c, qk_blocks + h)),
                pl.BlockSpec((1, L, P * RET_DV), lambda b, h, c: (b, c, v0 + h)),
                pl.BlockSpec((1, L, P * RET_DV), lambda b, h, c: (b, c, v0 + qk_blocks + h)),
                pl.BlockSpec((P, L, L), lambda b, h, c: (h, 0, 0)),
                pl.BlockSpec((P, L, 1), lambda b, h, c: (h, 0, 0)),
                pl.BlockSpec((P, L, 1), lambda b, h, c: (h, 0, 0)),
                pl.BlockSpec((P, 1, 1), lambda b, h, c: (h, 0, 0)),
                pl.BlockSpec((1, 1, P * RET_DV), lambda b, h, c: (l, 0, h))]
    args = [qkvg, qkvg, qkvg, qkvg, dec, qd, kd, gl, gn.reshape(gn.shape[0], 1, -1)]
    aliases = _chain(states, in_specs, args, 1)
    return pl.pallas_call(
        _ret_kernel,
        grid=(B, H // P, T // L),
        in_specs=in_specs,
        out_specs=[pl.BlockSpec((1, L, P * RET_DV), lambda b, h, c: (b, c, h)),
                   pl.BlockSpec((1, 1, P, RET_DK, RET_DV), lambda b, h, c: (l, b, h, 0, 0))],
        out_shape=[jax.ShapeDtypeStruct((B, T, H * RET_DV), BF16),
                   jax.ShapeDtypeStruct((N_A, B, H, RET_DK, RET_DV), F32)],
        input_output_aliases=aliases,
        compiler_params=_params(("arbitrary", "arbitrary", "arbitrary")),
        name="retention_prompt",
    )(*args)


def _ret_step_kernel(x_ref, s_ref, dec_ref, qd_ref, kd_ref, gl_ref, gn_ref, *rest, seq):
    o_ref, so_ref = rest[-2:]
    H = RET_HEADS
    zpad = jnp.zeros((RET_PAD - seq, RET_DV), F32)

    def padded(col0, width):
        return jnp.concatenate([x_ref[:, col0:col0 + width], zpad[:, :width]], axis=0)

    for h in range(H):
        q = padded(h * RET_DK, RET_DK)
        k = padded(H * RET_DK + h * RET_DK, RET_DK)
        v = padded(2 * H * RET_DK + h * RET_DV, RET_DV)
        gate = padded(2 * H * RET_DK + H * RET_DV + h * RET_DV, RET_DV)
        o, s_new = _retention_head(q, k, v, gate, s_ref[0, 0, h], dec_ref[h], qd_ref[h], kd_ref[h],
                                   gl_ref[h], gn_ref[0, :, h * RET_DV:(h + 1) * RET_DV])
        so_ref[0, 0, h] = s_new
        o_ref[:, h * RET_DV:(h + 1) * RET_DV] = o[:seq]


def retention_step(qkvg, state, gn, l, states, *, seq):
    rows, width = qkvg.shape
    nb = rows // seq
    H = RET_HEADS
    dec, qd, kd, gl = _decay_tables(seq, RET_PAD)
    whole = lambda shape: pl.BlockSpec(shape, lambda b: (0,) * len(shape))
    in_specs = [pl.BlockSpec((seq, width), lambda b: (b, 0)),
                pl.BlockSpec((1, 1, H, RET_DK, RET_DV), lambda b: (l, b, 0, 0, 0)),
                whole((H, RET_PAD, RET_PAD)), whole((H, RET_PAD, 1)), whole((H, RET_PAD, 1)), whole((H, 1, 1)),
                pl.BlockSpec((1, 1, H * RET_DV), lambda b: (l, 0, 0))]
    args = [qkvg, state, dec, qd, kd, gl, gn.reshape(gn.shape[0], 1, -1)]
    aliases = _chain(states, in_specs, args, 1)
    return pl.pallas_call(
        functools.partial(_ret_step_kernel, seq=seq),
        grid=(nb,),
        in_specs=in_specs,
        out_specs=[pl.BlockSpec((seq, H * RET_DV), lambda b: (b, 0)),
                   pl.BlockSpec((1, 1, H, RET_DK, RET_DV), lambda b: (l, b, 0, 0, 0))],
        out_shape=[jax.ShapeDtypeStruct((rows, H * RET_DV), F32),
                   jax.ShapeDtypeStruct(state.shape, F32)],
        input_output_aliases=aliases,
        compiler_params=_params(("arbitrary",)),
        name="retention_step",
    )(*args)


def _head_pairs(kcat, vcat):
    lo = lax.broadcasted_iota(jnp.int32, (kcat.shape[0], LANES), 1) < HEAD_DIM
    for pair in range(N_KV_HEADS // 2):
        kp = kcat[:, pair * LANES:(pair + 1) * LANES]
        vp = vcat[:, pair * LANES:(pair + 1) * LANES]
        kr = pltpu.roll(kp, HEAD_DIM, axis=1)
        vr = pltpu.roll(vp, HEAD_DIM, axis=1)
        for sub in range(2):
            ka, kb_ = (kp, kr) if sub == 0 else (kr, kp)
            va, vb_ = (vp, vr) if sub == 0 else (vr, vp)
            halves = [(jnp.where(lo, ka, 0.0).astype(BF16), jnp.where(lo, va, 0.0).astype(BF16)),
                      (jnp.where(lo, 0.0, kb_).astype(BF16), jnp.where(lo, 0.0, vb_).astype(BF16))]
            yield pair * 2 + sub, halves


def _sink_softmax(s, ok, sink):
    s = jnp.where(ok, s * (HEAD_DIM ** -0.5), NEG_INF)
    m = jnp.maximum(jnp.max(s, axis=-1, keepdims=True), sink)
    p = jnp.exp(s - m)
    return p * (1.0 / (jnp.sum(p, axis=-1, keepdims=True) + jnp.exp(sink - m)))


def _attend(q_of, kcat, vcat, ok, sink_ref, sink0, rows, store):
    pieces = GQA_GROUPS // 2
    for kh, halves in _head_pairs(kcat, vcat):
        base = kh * GQA_GROUPS * HEAD_DIM
        qs = jnp.concatenate([q_of(base + g * LANES) for g in range(pieces)], axis=0).astype(BF16)
        out = None
        for half, (kk, vv) in enumerate(halves):
            s = _bdot_nt(qs, kk)
            p = jnp.concatenate(
                [_sink_softmax(s[g * rows:(g + 1) * rows], ok, sink_ref[sink0 + kh * GQA_GROUPS + 2 * g + half])
                 for g in range(pieces)], axis=0)
            o = _bdot(p, vv)
            out = o if out is None else out + o
        for g in range(pieces):
            store(base + g * LANES, out[g * rows:(g + 1) * rows])


def _swa_kernel(sink_ref, q_ref, kp_ref, kc_ref, vp_ref, vc_ref, o_ref, *, sink0):
    i = pl.program_id(1)
    kcat = jnp.concatenate([kp_ref[0], kc_ref[0]], axis=0)
    vcat = jnp.concatenate([vp_ref[0], vc_ref[0]], axis=0)
    ql = lax.broadcasted_iota(jnp.int32, (ATT_BLOCK, 2 * ATT_BLOCK), 0)
    km = lax.broadcasted_iota(jnp.int32, (ATT_BLOCK, 2 * ATT_BLOCK), 1)
    no_prev = jnp.where(i > 0, 0, 4 * ATT_BLOCK)
    ok = ((km < ATT_BLOCK) & (km >= ql + no_prev)) | ((km >= ATT_BLOCK) & (km - ATT_BLOCK <= ql))

    def store(col0, val):
        o_ref[0, :, col0:col0 + LANES] = val.astype(o_ref.dtype)

    _attend(lambda c0: q_ref[0, :, c0:c0 + LANES], kcat, vcat, ok, sink_ref, sink0, ATT_BLOCK, store)


def swa_prompt(q, kv, sinks, j):
    B, T, DQ = q.shape
    DKV = kv.shape[2] // 2
    cur = lambda b, i: (b, i, 0)
    return pl.pallas_call(
        functools.partial(_swa_kernel, sink0=j * N_Q_HEADS),
        grid=(B, T // ATT_BLOCK),
        in_specs=[pl.BlockSpec(memory_space=pltpu.SMEM),
                  pl.BlockSpec((1, ATT_BLOCK, DQ), cur),
                  pl.BlockSpec((1, ATT_BLOCK, DKV), lambda b, i: (b, jnp.maximum(i - 1, 0), 0)),
                  pl.BlockSpec((1, ATT_BLOCK, DKV), lambda b, i: (b, i, 0)),
                  pl.BlockSpec((1, ATT_BLOCK, DKV), lambda b, i: (b, jnp.maximum(i - 1, 0), 1)),
                  pl.BlockSpec((1, ATT_BLOCK, DKV), lambda b, i: (b, i, 1))],
        out_specs=pl.BlockSpec((1, ATT_BLOCK, DQ), cur),
        out_shape=jax.ShapeDtypeStruct((B, T, DQ), BF16),
        compiler_params=_params(("arbitrary", "arbitrary")),
        name="swa_prompt",
    )(sinks, q, kv, kv, kv, kv)


def _swa_step_kernel(sink_ref, q_ref, ck_ref, nk_ref, cv_ref, nv_ref, o_ref, *, seq, wb, sink0):
    zpad = jnp.zeros((2 * ATT_BLOCK - wb - seq, nk_ref.shape[1]), F32)
    kcat = jnp.concatenate([ck_ref[0], nk_ref[...], zpad], axis=0)
    vcat = jnp.concatenate([cv_ref[0], nv_ref[...], zpad], axis=0)
    nk = kcat.shape[0]
    t = lax.broadcasted_iota(jnp.int32, (seq, nk), 0)
    s = lax.broadcasted_iota(jnp.int32, (seq, nk), 1)
    rel = t + wb - s
    ok = (rel >= 0) & (rel <= WINDOW)

    def store(col0, val):
        o_ref[:, col0:col0 + LANES] = val

    _attend(lambda c0: q_ref[:, c0:c0 + LANES], kcat, vcat, ok, sink_ref, sink0, seq, store)


def swa_step(q, cache_k, cache_v, kv_new, sinks, j, *, seq):
    rows, DQ = q.shape
    nb, wb, DKV = cache_k.shape
    return pl.pallas_call(
        functools.partial(_swa_step_kernel, seq=seq, wb=wb, sink0=j * N_Q_HEADS),
        grid=(nb,),
        in_specs=[pl.BlockSpec(memory_space=pltpu.SMEM),
                  pl.BlockSpec((seq, DQ), lambda b: (b, 0)),
                  pl.BlockSpec((1, wb, DKV), lambda b: (b, 0, 0)), pl.BlockSpec((seq, DKV), lambda b: (b, 0)),
                  pl.BlockSpec((1, wb, DKV), lambda b: (b, 0, 0)), pl.BlockSpec((seq, DKV), lambda b: (b, 1))],
        out_specs=pl.BlockSpec((seq, DQ), lambda b: (b, 0)),
        out_shape=jax.ShapeDtypeStruct((rows, DQ), F32),
        compiler_params=_params(("arbitrary",)),
        name="swa_step",
    )(sinks, q, cache_k, kv_new, cache_v, kv_new)


def _rope_tables_partial(pos):
    half = ROPE_DIM // 2
    inv = ROPE_THETA ** (-jnp.arange(half, dtype=F32) * 2.0 / ROPE_DIM)
    ang = pos.astype(F32)[:, None] * inv[None, :]
    cos, sin = jnp.cos(ang), jnp.sin(ang)
    T = pos.shape[0]
    ones = jnp.ones((T, HEAD_DIM - ROPE_DIM), F32)
    zeros = jnp.zeros((T, HEAD_DIM - ROPE_DIM), F32)
    zh = jnp.zeros((T, half), F32)
    c = jnp.concatenate([cos, cos, ones], axis=1)
    s1 = jnp.concatenate([-sin, zh, zeros], axis=1)
    s2 = jnp.concatenate([zh, sin, zeros], axis=1)
    rep = LANES // HEAD_DIM
    return tuple(jnp.tile(a, (1, rep)) for a in (c, s1, s2))


def _final_norm_kernel(x_ref, g_ref, o_ref):
    x = x_ref[0]
    ms = jnp.mean(x * x, axis=-1, keepdims=True)
    o_ref[0] = x * lax.rsqrt(ms + EPS) * g_ref[...]


def final_norm(x, g, tm):
    B, T, D = x.shape
    return pl.pallas_call(
        _final_norm_kernel,
        grid=(B, T // tm),
        in_specs=[pl.BlockSpec((1, tm, D), lambda b, i: (b, i, 0)), pl.BlockSpec((1, D), lambda b, i: (0, 0))],
        out_specs=pl.BlockSpec((1, tm, D), lambda b, i: (b, i, 0)),
        out_shape=jax.ShapeDtypeStruct((B, T, D), F32),
        compiler_params=_params(("arbitrary", "arbitrary")),
        name="final_norm",
    )(x, g.reshape(1, D))


def kernel(x_prompt, x_sample, state_ret, cache_win_k, cache_win_v, state_conv, c_prompt, c_sample,
           w_ada, b_ada, norm_mix, norm_ffn, ret_w_in, ret_gn, ret_w_out,
           kv_norm, kv_w_ada, kv_b_ada, w_kv, att_w_q, att_sinks, att_w_o,
           ffn_w_up, ffn_conv_w, ffn_conv_b, ffn_w_down, norm_f):
    D = D_MODEL
    BP, TP, _ = x_prompt.shape
    BS, TS, _ = x_sample.shape
    RS = BS * TS
    KV = N_KV_HEADS * HEAD_DIM
    QK = RET_HEADS * RET_DK

    c_all = jnp.concatenate([c_sample, c_prompt], axis=0)
    c_all = jnp.pad(c_all, ((0, -c_all.shape[0] % PACK), (0, 0)))
    mods = Mods(ada_mods(c_all, w_ada, b_ada), BS)
    kv_mods = Mods(ada_mods(c_all, kv_w_ada[None], kv_b_ada[None]), BS)
    tile_col = lambda b, i, j: j

    pos_p = jnp.arange(TP, dtype=jnp.int32)
    pos_s = PAST_LEN + jnp.arange(TS, dtype=jnp.int32)
    tabs_p = _rope_tables_partial(pos_p)
    tabs_s = tuple(jnp.tile(a, (BS, 1)) for a in _rope_tables_partial(pos_s))
    full_p = _rope_tables_full(pos_p)
    full_s = tuple(jnp.tile(a, (BS, 1)) for a in _rope_tables_full(pos_s))
    sinks = att_sinks.reshape(-1)
    kv_norm1, w_kv1 = kv_norm[None], w_kv[None]

    x = x_prompt
    ret_p = None
    conv_p = []
    for l in range(DEPTH):
        if l == N_A:
            kv_p = proj(x, kv_norm1, 0, kv_mods.prompt(0, 0, D), kv_mods.prompt(0, 1, D), w_kv1, 0, tm=1024, tn=2 * KV,
                        tok=None, out_dtype=F32, rope="partial", rope_tabs=tabs_p, rope_cols=KV)
        sh1, sc1, sh2, sc2 = (mods.prompt(l, k, D) for k in (0, 1, 3, 4))
        if l < N_A:
            qkvg = proj(x, norm_mix, l, sh1, sc1, ret_w_in, l, tm=1024, tn=512, tok=None, out_dtype=F32,
                        rope="full", rope_tabs=full_p, rope_cols=2 * QK, scale_from=QK)
            a, ret_p = retention_prompt(qkvg, ret_gn, l, ret_p)
            x = out_proj(a, ret_w_out, l, x, mods.prompt(l, 2, 256, tile_col), tm=1024, tn=256, tok=None)
        else:
            j = l - N_A
            q = proj(x, norm_mix, l, sh1, sc1, att_w_q, j, tm=1024, tn=512, tok=None, out_dtype=BF16,
                     rope="partial", rope_tabs=tabs_p, rope_cols=D)
            a = swa_prompt(q, kv_p, sinks, j)
            x = out_proj(a, att_w_o, j, x, mods.prompt(l, 2, 512, tile_col), tm=1024, tn=512, tok=None)
        z, st = upconv_prompt(x, norm_ffn, l, sh2, sc2, ffn_w_up, ffn_conv_w, ffn_conv_b)
        conv_p.append(st[:, -1, :, SUBLANES - (CONV_W - 1):, :].transpose(0, 2, 1, 3).reshape(BP, CONV_W - 1, 2 * D_FF))
        x = out_proj(z, ffn_w_down, l, x, mods.prompt(l, 5, 256, tile_col), tm=1024, tn=256, tok=None)
    y_prompt = final_norm(x, norm_f, 512)
    wp = min(WINDOW, TP)
    win_k_prompt = kv_p[:, -wp:, :KV].reshape(BP, wp, N_KV_HEADS, HEAD_DIM)
    win_v_prompt = kv_p[:, -wp:, KV:].reshape(BP, wp, N_KV_HEADS, HEAD_DIM)

    x = x_sample.reshape(1, RS, D)
    wb = cache_win_k.shape[1]
    ck = cache_win_k.reshape(BS, wb, KV)
    cv = cache_win_v.reshape(BS, wb, KV)
    ret_s = None
    conv_s = []
    for l in range(DEPTH):
        if l == N_A:
            kv_s = proj(x, kv_norm1, 0, kv_mods.sample(0, 0, D), kv_mods.sample(0, 1, D), w_kv1, 0, tm=RS, tn=2 * KV,
                        tok=TS, out_dtype=F32, rope="partial", rope_tabs=tabs_s, rope_cols=KV)[0]
        sh1, sc1, sh2, sc2 = (mods.sample(l, k, D) for k in (0, 1, 3, 4))
        if l < N_A:
            qkvg = proj(x, norm_mix, l, sh1, sc1, ret_w_in, l, tm=RS, tn=1024, tok=TS, out_dtype=F32,
                        rope="full", rope_tabs=full_s, rope_cols=2 * QK, scale_from=QK)[0]
            a, ret_s = retention_step(qkvg, state_ret, ret_gn, l, ret_s, seq=TS)
            x = out_proj(a[None], ret_w_out, l, x, mods.sample(l, 2, 512, tile_col), tm=RS, tn=512, tok=TS)
        else:
            j = l - N_A
            q = proj(x, norm_mix, l, sh1, sc1, att_w_q, j, tm=RS, tn=1024, tok=TS, out_dtype=F32,
                     rope="partial", rope_tabs=tabs_s, rope_cols=D)[0]
            a = swa_step(q, ck, cv, kv_s, sinks, j, seq=TS)
            x = out_proj(a[None], att_w_o, j, x, mods.sample(l, 2, 1024, tile_col), tm=RS, tn=1024, tok=TS)
        z, ua, ub = upconv_step(x, norm_ffn, l, sh2, sc2, ffn_w_up, ffn_conv_w, ffn_conv_b, state_conv, seq=TS)
        conv_s.append(jnp.concatenate([ua.reshape(BS, TS, D_FF)[:, TS - (CONV_W - 1):],
                                       ub.reshape(BS, TS, D_FF)[:, TS - (CONV_W - 1):]], axis=-1))
        x = out_proj(z[None], ffn_w_down, l, x, mods.sample(l, 5, 512, tile_col), tm=RS, tn=512, tok=TS)
    y_sample = final_norm(x, norm_f, RS).reshape(BS, TS, D)
    k_all = jnp.concatenate([ck, kv_s[:, :KV].reshape(BS, TS, KV)], axis=1)
    v_all = jnp.concatenate([cv, kv_s[:, KV:].reshape(BS, TS, KV)], axis=1)
    win_k_sample = k_all[:, -wb:].reshape(BS, wb, N_KV_HEADS, HEAD_DIM)
    win_v_sample = v_all[:, -wb:].reshape(BS, wb, N_KV_HEADS, HEAD_DIM)

    return (y_prompt, y_sample, ret_p, ret_s, win_k_prompt, win_v_prompt,
            win_k_sample, win_v_sample, jnp.stack(conv_p), jnp.stack(conv_s))
```

```python
import functools

import jax
import jax.numpy as jnp
from jax import lax
from jax.experimental import pallas as pl
from jax.experimental.pallas import tpu as pltpu

F32 = jnp.float32
BF16 = jnp.bfloat16

D_MODEL = 2048
DEPTH = 4
PAST_LEN = 16384
N_A = DEPTH // 2
RET_HEADS = 8
RET_DK = D_MODEL // RET_HEADS
RET_DV = 2 * D_MODEL // RET_HEADS
RET_ROPE_BASE = 10000.0
N_Q_HEADS = 32
N_KV_HEADS = 4
HEAD_DIM = D_MODEL // N_Q_HEADS
GQA_GROUPS = N_Q_HEADS // N_KV_HEADS
ROPE_DIM = HEAD_DIM // 4
ROPE_THETA = 500000.0
WINDOW = 128
ATT_BLOCK = 128
D_FF = 2 * D_MODEL
CONV_W = 3
N_MOD = 6
EPS = 1e-6
NEG_INF = -1e30

LANES = 128
SUBLANES = 8
PACK = 16
HALO = PACK
RET_BLOCK = 256
RET_HEADS_PER_STEP = 2
RET_PAD = PACK
VMEM_LIMIT = 56 * 1024 * 1024


def _params(sem):
    return pltpu.CompilerParams(dimension_semantics=sem, vmem_limit_bytes=VMEM_LIMIT)


def _silu(x):
    return x * jax.nn.sigmoid(x)


def _bdot(a, b):
    return jnp.dot(a.astype(BF16), b.astype(BF16), preferred_element_type=F32)


def _bdot_nt(a, b):
    return lax.dot_general(a.astype(BF16), b.astype(BF16), (((1,), (1,)), ((), ())), preferred_element_type=F32)


def _norm_mod(x, g, sc, sh):
    ms = jnp.mean(x * x, axis=-1, keepdims=True)
    y = x * lax.rsqrt(ms + EPS) * g
    return y * (1.0 + sc) + sh


class Mods:
    def __init__(self, arr, n_sample):
        self.a3 = arr
        self.a4 = arr.reshape(arr.shape[0], arr.shape[1], 1, arr.shape[2])
        self.n_sample = n_sample

    def prompt(self, l, k, width, col=lambda *g: 0):
        per = D_MODEL // width
        off = self.n_sample
        return self.a4, pl.BlockSpec((1, 1, 1, width), lambda *g: (l, off + g[0], 0, k * per + col(*g)))

    def sample(self, l, k, width, col=lambda *g: 0):
        per = D_MODEL // width
        return self.a3, pl.BlockSpec((1, self.n_sample, width), lambda *g: (l, 0, k * per + col(*g)))


def _ada_kernel(c_ref, w_ref, b_ref, o_ref):
    o_ref[0] = _bdot(_silu(c_ref[...]), w_ref[0]) + b_ref[0]


def ada_mods(c_all, w, b, tn=1024):
    L, D, N = w.shape
    R = c_all.shape[0]
    return pl.pallas_call(
        _ada_kernel,
        grid=(L, N // tn),
        in_specs=[pl.BlockSpec((R, D), lambda l, j: (0, 0)),
                  pl.BlockSpec((1, D, tn), lambda l, j: (l, 0, j)),
                  pl.BlockSpec((1, 1, tn), lambda l, j: (l, 0, j))],
        out_specs=pl.BlockSpec((1, R, tn), lambda l, j: (l, 0, j)),
        out_shape=jax.ShapeDtypeStruct((L, R, N), F32),
        compiler_params=_params(("arbitrary", "arbitrary")),
        name="ada_mods",
    )(c_all, w, b.reshape(L, 1, N))


def _store_normed(h_scr, r_scr, row0, x_ref, g_ref, sc_ref, sh_ref, rows, tok):
    def stats(r, carry):
        r0 = pl.multiple_of(r * PACK, PACK)
        x = x_ref[0, pl.ds(r0, PACK), :]
        r_scr[pl.ds(r0, PACK), :] = lax.rsqrt(jnp.mean(x * x, axis=-1, keepdims=True) + EPS)
        return carry

    lax.fori_loop(0, rows // PACK, stats, 0, unroll=4)
    g = g_ref[0]

    def piece(start, n, sc, sh):
        return x_ref[0, pl.ds(start, n), :] * r_scr[pl.ds(start, n), :] * g * (1.0 + sc) + sh

    def apply(r, carry):
        r0 = pl.multiple_of(r * PACK, PACK)
        if tok is None:
            h = piece(r0, PACK, sc_ref[0, 0], sh_ref[0, 0])
        else:
            parts = []
            for s in range(PACK // tok):
                m = r * (PACK // tok) + s
                start = pl.multiple_of(r0 + s * tok, tok)
                parts.append(piece(start, tok, sc_ref[0, pl.ds(m, 1), :], sh_ref[0, pl.ds(m, 1), :]))
            h = jnp.concatenate(parts, axis=0)
        h_scr[pl.ds(pl.multiple_of(row0 + r0, PACK), PACK), :] = h.astype(BF16)
        return carry

    lax.fori_loop(0, rows // PACK, apply, 0, unroll=2)


def _rope64(a, c, s1, s2):
    return a * c + pltpu.roll(a, LANES - ROPE_DIM // 2, axis=1) * s1 + pltpu.roll(a, ROPE_DIM // 2, axis=1) * s2


def _proj_kernel(*refs, tm, tn, tok, rope, rope_cols, scale_from, emit_w):
    x_ref, g_ref, sh_ref, sc_ref, w_ref = refs[:5]
    if rope:
        t1_ref, t2_ref, t3_ref = refs[5:8]
    o_ref = refs[-4] if emit_w else refs[-3]
    h_scr, r_scr = refs[-2:]
    j = pl.program_id(2)

    @pl.when(j == 0)
    def _():
        _store_normed(h_scr, r_scr, 0, x_ref, g_ref, sc_ref, sh_ref, tm, tok)

    wb = w_ref[0].astype(BF16)
    if emit_w:
        refs[-3][0] = wb
    acc = jnp.dot(h_scr[...], wb, preferred_element_type=F32)

    def plain():
        o_ref[0] = acc.astype(o_ref.dtype)

    if rope is None:
        plain()
    elif rope == "partial":
        for c in range(tn // LANES):
            a = acc[:, c * LANES:(c + 1) * LANES]
            if c * LANES < rope_cols:
                a = _rope64(a, t1_ref[...], t2_ref[...], t3_ref[...])
            o_ref[0, :, c * LANES:(c + 1) * LANES] = a.astype(o_ref.dtype)
    else:
        @pl.when(j * tn < rope_cols)
        def _():
            cos, sin = t1_ref[...], t2_ref[...]
            scale = jnp.where(j * tn >= scale_from, RET_DK ** -0.5, 1.0).astype(F32)
            for hd in range(tn // RET_DK):
                x1 = acc[:, hd * RET_DK:hd * RET_DK + LANES]
                x2 = acc[:, hd * RET_DK + LANES:(hd + 1) * RET_DK]
                o_ref[0, :, hd * RET_DK:hd * RET_DK + LANES] = ((x1 * cos - x2 * sin) * scale).astype(o_ref.dtype)
                o_ref[0, :, hd * RET_DK + LANES:(hd + 1) * RET_DK] = ((x2 * cos + x1 * sin) * scale).astype(o_ref.dtype)

        pl.when(j * tn >= rope_cols)(plain)


def proj(x, g, l, sh, sc, w, wl, *, tm, tn, tok, out_dtype, rope=None, rope_tabs=None, rope_cols=0, scale_from=0,
         emit_w=False):
    B, T, D = x.shape
    N = w.shape[2]
    assert rope != "partial" or rope_cols == N or tn == N
    assert not emit_w or (B == 1 and T == tm)
    out_specs = [pl.BlockSpec((1, tm, tn), lambda b, i, j: (b, i, j))]
    out_shape = [jax.ShapeDtypeStruct((B, T, N), out_dtype)]
    if emit_w:
        out_specs.append(pl.BlockSpec((1, D, tn), lambda b, i, j: (0, 0, j)))
        out_shape.append(jax.ShapeDtypeStruct((1, D, N), BF16))
    in_specs = [pl.BlockSpec((1, tm, D), lambda b, i, j: (b, i, 0)),
                pl.BlockSpec((1, 1, D), lambda b, i, j: (l, 0, 0)),
                sh[1], sc[1],
                pl.BlockSpec((1, D, tn), lambda b, i, j: (wl, 0, j))]
    args = [x, g.reshape(g.shape[0], 1, D), sh[0], sc[0], w]
    if rope:
        tabs = list(rope_tabs) + ([rope_tabs[0]] if len(rope_tabs) == 2 else [])
        in_specs += [pl.BlockSpec((tm, LANES), lambda b, i, j: (i, 0))] * 3
        args += tabs
    outs = pl.pallas_call(
        functools.partial(_proj_kernel, tm=tm, tn=tn, tok=tok, rope=rope, rope_cols=rope_cols, scale_from=scale_from,
                          emit_w=emit_w),
        grid=(B, T // tm, N // tn),
        in_specs=in_specs,
        out_specs=out_specs,
        out_shape=out_shape,
        scratch_shapes=[pltpu.VMEM((tm, D), BF16), pltpu.VMEM((tm, 1), F32)],
        compiler_params=_params(("arbitrary", "arbitrary", "arbitrary")),
        name="proj",
    )(*args)
    return outs if emit_w else outs[0]


def _conv_gate(ua, ub, cwa_ref, cwb_ref, cba_ref, cbb_ref):
    def conv(u, cw_ref, cb_ref):
        acc = cb_ref[0] + cw_ref[0, 0:1, :] * u[0]
        acc = acc + cw_ref[0, 1:2, :] * u[1]
        return acc + cw_ref[0, 2:3, :] * u[2]

    return _silu(conv(ua, cwa_ref, cba_ref)) * conv(ub, cwb_ref, cbb_ref)


def _upconv_kernel(x_ref, xh_ref, g_ref, sh_ref, sc_ref, wa_ref, wb_ref, cwa_ref, cwb_ref, cba_ref, cbb_ref,
                   z_ref, st_ref, h_scr, r_scr, *, tm):
    i = pl.program_id(1)

    @pl.when(pl.program_id(2) == 0)
    def _():
        _store_normed(h_scr, r_scr, HALO, x_ref, g_ref, sc_ref, sh_ref, tm, None)
        hh = _norm_mod(xh_ref[0], g_ref[0], sc_ref[0, 0], sh_ref[0, 0])
        h_scr[0:HALO, :] = jnp.where(i > 0, hh, 0.0).astype(BF16)

    h = h_scr[...]

    def taps(w_ref):
        u = _bdot(h, w_ref[0])
        return (pltpu.roll(u, 2, axis=0)[HALO:], pltpu.roll(u, 1, axis=0)[HALO:], u[HALO:])

    ua = taps(wa_ref)
    ub = taps(wb_ref)
    z_ref[0] = _conv_gate(ua, ub, cwa_ref, cwb_ref, cba_ref, cbb_ref).astype(z_ref.dtype)
    st_ref[0, 0, 0] = ua[2][tm - SUBLANES:, :]
    st_ref[0, 0, 1] = ub[2][tm - SUBLANES:, :]


def _halves_specs(shape, l, nj, col):
    return [pl.BlockSpec(shape, lambda *g: (l, 0, col(*g))), pl.BlockSpec(shape, lambda *g: (l, 0, col(*g) + nj))]


def upconv_prompt(x, g, l, sh, sc, wa, wb, conv_w, conv_b, *, tm=1024, tn=512):
    B, T, D = x.shape
    F = wa.shape[2]
    nj = F // tn
    L = conv_w.shape[0]
    col = lambda b, i, j: j
    in_specs = [pl.BlockSpec((1, tm, D), lambda b, i, j: (b, i, 0)),
                pl.BlockSpec((1, HALO, D), lambda b, i, j: (b, jnp.maximum(i * (tm // HALO) - 1, 0), 0)),
                pl.BlockSpec((1, 1, D), lambda b, i, j: (l, 0, 0)),
                sh[1], sc[1],
                pl.BlockSpec((1, D, tn), lambda b, i, j: (0, 0, j)),
                pl.BlockSpec((1, D, tn), lambda b, i, j: (0, 0, j))]
    in_specs += _halves_specs((1, CONV_W, tn), l, nj, col) + _halves_specs((1, 1, tn), l, nj, col)
    cb = conv_b.reshape(L, 1, 2 * F)
    return pl.pallas_call(
        functools.partial(_upconv_kernel, tm=tm),
        grid=(B, T // tm, nj),
        in_specs=in_specs,
        out_specs=[pl.BlockSpec((1, tm, tn), lambda b, i, j: (b, i, j)),
                   pl.BlockSpec((1, 1, 2, SUBLANES, tn), lambda b, i, j: (b, i, 0, 0, j))],
        out_shape=[jax.ShapeDtypeStruct((B, T, F), BF16),
                   jax.ShapeDtypeStruct((B, T // tm, 2, SUBLANES, F), F32)],
        scratch_shapes=[pltpu.VMEM((HALO + tm, D), BF16), pltpu.VMEM((tm, 1), F32)],
        compiler_params=_params(("arbitrary", "arbitrary", "arbitrary")),
        name="upconv_prompt",
    )(x, x, g.reshape(L, 1, D), sh[0], sc[0], wa, wb, conv_w, conv_w, cb, cb)


def _upconv_step_kernel(x_ref, g_ref, sh_ref, sc_ref, wa_ref, wb_ref, cwa_ref, cwb_ref, cba_ref, cbb_ref,
                        sta_ref, stb_ref, z_ref, ua_ref, ub_ref, wao_ref, wbo_ref, h_scr, r_scr, *, rows, seq):
    @pl.when(pl.program_id(0) == 0)
    def _():
        _store_normed(h_scr, r_scr, 0, x_ref, g_ref, sc_ref, sh_ref, rows, seq)

    h = h_scr[...]
    nb = rows // seq

    def taps(w_ref, st_ref, u_ref, wo_ref):
        w = w_ref[0].astype(BF16)
        wo_ref[0] = w
        u = jnp.dot(h, w, preferred_element_type=F32)
        u_ref[...] = u
        u3 = u.reshape(nb, seq, u.shape[1])
        t = lax.broadcasted_iota(jnp.int32, u3.shape, 1)
        st0 = st_ref[0, :, 0:1, :]
        st1 = st_ref[0, :, 1:2, :]
        um1 = jnp.where(t == 0, st1, pltpu.roll(u3, 1, axis=1))
        um2 = jnp.where(t == 0, st0, jnp.where(t == 1, st1, pltpu.roll(u3, 2, axis=1)))
        return (um2, um1, u3)

    ua = taps(wa_ref, sta_ref, ua_ref, wao_ref)
    ub = taps(wb_ref, stb_ref, ub_ref, wbo_ref)
    z = _conv_gate(ua, ub, cwa_ref, cwb_ref, cba_ref, cbb_ref)
    z_ref[...] = z.reshape(rows, z.shape[2]).astype(z_ref.dtype)


def upconv_step(x, g, l, sh, sc, w_up, conv_w, conv_b, state, *, seq, tn=512):
    _, rows, D = x.shape
    F = w_up.shape[2] // 2
    nj = F // tn
    nb = rows // seq
    L = w_up.shape[0]
    col = lambda j: j
    in_specs = [pl.BlockSpec((1, rows, D), lambda j: (0, 0, 0)),
                pl.BlockSpec((1, 1, D), lambda j: (l, 0, 0)),
                sh[1], sc[1]]
    in_specs += _halves_specs((1, D, tn), l, nj, col) + _halves_specs((1, CONV_W, tn), l, nj, col)
    in_specs += _halves_specs((1, 1, tn), l, nj, col)
    in_specs += [pl.BlockSpec((1, nb, CONV_W - 1, tn), lambda j: (l, 0, 0, j)),
                 pl.BlockSpec((1, nb, CONV_W - 1, tn), lambda j: (l, 0, 0, j + nj))]
    cb = conv_b.reshape(L, 1, 2 * F)
    return pl.pallas_call(
        functools.partial(_upconv_step_kernel, rows=rows, seq=seq),
        grid=(nj,),
        in_specs=in_specs,
        out_specs=[pl.BlockSpec((rows, tn), lambda j: (0, j))] * 3 + [pl.BlockSpec((1, D, tn), lambda j: (0, 0, j))] * 2,
        out_shape=[jax.ShapeDtypeStruct((rows, F), BF16), jax.ShapeDtypeStruct((rows, F), F32),
                   jax.ShapeDtypeStruct((rows, F), F32), jax.ShapeDtypeStruct((1, D, F), BF16),
                   jax.ShapeDtypeStruct((1, D, F), BF16)],
        scratch_shapes=[pltpu.VMEM((rows, D), BF16), pltpu.VMEM((rows, 1), F32)],
        compiler_params=_params(("arbitrary",)),
        name="upconv_step",
    )(x, g.reshape(L, 1, D), sh[0], sc[0], w_up, w_up, conv_w, conv_w, cb, cb, state, state)


def _out_kernel(a_ref, w_ref, x_ref, gt_ref, o_ref, *wo_ref, tok):
    w = w_ref[0].astype(BF16)
    if wo_ref:
        wo_ref[0][0] = w
    y = jnp.dot(a_ref[0].astype(BF16), w, preferred_element_type=F32)
    if tok is None:
        o_ref[0] = x_ref[0] + gt_ref[0, 0] * y
    else:
        for b in range(y.shape[0] // tok):
            rows = slice(b * tok, (b + 1) * tok)
            o_ref[0, rows, :] = x_ref[0, rows, :] + gt_ref[0, b:b + 1, :] * y[rows]


def out_proj(a, w, l, x, gate, *, tm, tn, tok, emit_w=False):
    B, T, K = a.shape
    N = w.shape[2]
    assert not emit_w or (B == 1 and T == tm)
    out_specs = [pl.BlockSpec((1, tm, tn), lambda b, i, j: (b, i, j))]
    out_shape = [jax.ShapeDtypeStruct((B, T, N), F32)]
    if emit_w:
        out_specs.append(pl.BlockSpec((1, K, tn), lambda b, i, j: (0, 0, j)))
        out_shape.append(jax.ShapeDtypeStruct((1, K, N), BF16))
    outs = pl.pallas_call(
        functools.partial(_out_kernel, tok=tok),
        grid=(B, T // tm, N // tn),
        in_specs=[pl.BlockSpec((1, tm, K), lambda b, i, j: (b, i, 0)),
                  pl.BlockSpec((1, K, tn), lambda b, i, j: (l, 0, j)),
                  pl.BlockSpec((1, tm, tn), lambda b, i, j: (b, i, j)),
                  gate[1]],
        out_specs=out_specs,
        out_shape=out_shape,
        compiler_params=_params(("arbitrary", "arbitrary", "arbitrary")),
        name="out_proj",
    )(a, w, x, gate[0])
    return outs if emit_w else outs[0]


def _rope_full(x, cos, sin):
    x1 = x[:, :LANES]
    x2 = x[:, LANES:]
    return jnp.concatenate([x1 * cos - x2 * sin, x2 * cos + x1 * sin], axis=1)


def _retention_head(q, k, v, gate, S, dec, qd, kd, gl, gn):
    qb = q.astype(BF16)
    vb = v.astype(BF16)
    scores = _bdot_nt(qb, k) * dec
    o = _bdot(scores, vb) + _bdot(qb, S) * qd
    kdt = (k * kd).T
    s_new = gl * S + _bdot(kdt, vb)
    mu = jnp.mean(o, axis=-1, keepdims=True)
    d = o - mu
    var = jnp.mean(d * d, axis=-1, keepdims=True)
    on = d * lax.rsqrt(var + EPS) * gn
    return _silu(gate) * on, s_new


def _ret_kernel(q_ref, k_ref, v_ref, g_ref, dec_ref, qd_ref, kd_ref, gl_ref, gn_ref, *rest):
    o_ref, s_ref = rest[-2:]

    @pl.when(pl.program_id(2) == 0)
    def _():
        s_ref[...] = jnp.zeros_like(s_ref)

    for hd in range(RET_HEADS_PER_STEP):
        kc = slice(hd * RET_DK, (hd + 1) * RET_DK)
        vc = slice(hd * RET_DV, (hd + 1) * RET_DV)
        o, s_new = _retention_head(q_ref[0, :, kc], k_ref[0, :, kc], v_ref[0, :, vc], g_ref[0, :, vc],
                                   s_ref[0, 0, hd], dec_ref[hd], qd_ref[hd], kd_ref[hd], gl_ref[hd], gn_ref[0, :, vc])
        s_ref[0, 0, hd] = s_new
        o_ref[0, :, vc] = o.astype(o_ref.dtype)


def _decay_tables(L, Lp):
    log_gamma = jnp.log(1.0 - jnp.exp2(-5.0 - jnp.arange(RET_HEADS, dtype=F32)))
    idx = jnp.arange(L, dtype=F32)
    rel = idx[:, None] - idx[None, :]
    dec = jnp.where(rel >= 0, jnp.exp(jnp.maximum(rel, 0.0)[None] * log_gamma[:, None, None]), 0.0)
    qd = jnp.exp((idx + 1.0)[None, :] * log_gamma[:, None])[..., None]
    kd = jnp.exp((L - 1.0 - idx)[None, :] * log_gamma[:, None])[..., None]
    gl = jnp.exp(L * log_gamma).reshape(RET_HEADS, 1, 1)
    p = Lp - L
    return (jnp.pad(dec, ((0, 0), (0, p), (0, p))), jnp.pad(qd, ((0, 0), (0, p), (0, 0))),
            jnp.pad(kd, ((0, 0), (0, p), (0, 0))), gl)


def _rope_tables_full(pos):
    half = RET_DK // 2
    inv = RET_ROPE_BASE ** (-jnp.arange(half, dtype=F32) * 2.0 / RET_DK)
    ang = pos.astype(F32)[:, None] * inv[None, :]
    return jnp.cos(ang), jnp.sin(ang)


def _chain(prev, in_specs, args, out_index):
    if prev is None:
        return {}
    in_specs.append(pl.BlockSpec(memory_space=pl.ANY))
    args.append(prev)
    return {len(args) - 1: out_index}


def retention_prompt(qkvg, gn, l, states):
    B, T, _ = qkvg.shape
    L = RET_BLOCK
    H = RET_HEADS
    P = RET_HEADS_PER_STEP
    dec, qd, kd, gl = _decay_tables(L, L)
    qk_blocks = H // P
    v0 = 2 * H * RET_DK // (P * RET_DV)
    in_specs = [pl.BlockSpec((1, L, P * RET_DK), lambda b, h, c: (b, c, h)),
                pl.BlockSpec((1, L, P * RET_DK), lambda b, h, c: (b, c, qk_blocks + h)),
                pl.BlockSpec((1, L, P * RET_DV), lambda b, h, c: (b, c, v0 + h)),
                pl.BlockSpec((1, L, P * RET_DV), lambda b, h, c: (b, c, v0 + qk_blocks + h)),
                pl.BlockSpec((P, L, L), lambda b, h, c: (h, 0, 0)),
                pl.BlockSpec((P, L, 1), lambda b, h, c: (h, 0, 0)),
                pl.BlockSpec((P, L, 1), lambda b, h, c: (h, 0, 0)),
                pl.BlockSpec((P, 1, 1), lambda b, h, c: (h, 0, 0)),
                pl.BlockSpec((1, 1, P * RET_DV), lambda b, h, c: (l, 0, h))]
    args = [qkvg, qkvg, qkvg, qkvg, dec, qd, kd, gl, gn.reshape(gn.shape[0], 1, -1)]
    aliases = _chain(states, in_specs, args, 1)
    return pl.pallas_call(
        _ret_kernel,
        grid=(B, H // P, T // L),
        in_specs=in_specs,
        out_specs=[pl.BlockSpec((1, L, P * RET_DV), lambda b, h, c: (b, c, h)),
                   pl.BlockSpec((1, 1, P, RET_DK, RET_DV), lambda b, h, c: (l, b, h, 0, 0))],
        out_shape=[jax.ShapeDtypeStruct((B, T, H * RET_DV), BF16),
                   jax.ShapeDtypeStruct((N_A, B, H, RET_DK, RET_DV), F32)],
        input_output_aliases=aliases,
        compiler_params=_params(("arbitrary", "arbitrary", "arbitrary")),
        name="retention_prompt",
    )(*args)


def _ret_step_kernel(x_ref, s_ref, dec_ref, qd_ref, kd_ref, gl_ref, gn_ref, *rest, seq):
    o_ref, so_ref = rest[-2:]
    H = RET_HEADS
    zpad = jnp.zeros((RET_PAD - seq, RET_DV), F32)

    def padded(col0, width):
        return jnp.concatenate([x_ref[:, col0:col0 + width], zpad[:, :width]], axis=0)

    for h in range(H):
        q = padded(h * RET_DK, RET_DK)
        k = padded(H * RET_DK + h * RET_DK, RET_DK)
        v = padded(2 * H * RET_DK + h * RET_DV, RET_DV)
        gate = padded(2 * H * RET_DK + H * RET_DV + h * RET_DV, RET_DV)
        o, s_new = _retention_head(q, k, v, gate, s_ref[0, 0, h], dec_ref[h], qd_ref[h], kd_ref[h],
                                   gl_ref[h], gn_ref[0, :, h * RET_DV:(h + 1) * RET_DV])
        so_ref[0, 0, h] = s_new
        o_ref[:, h * RET_DV:(h + 1) * RET_DV] = o[:seq]


def retention_step(qkvg, state, gn, l, states, *, seq):
    rows, width = qkvg.shape
    nb = rows // seq
    H = RET_HEADS
    dec, qd, kd, gl = _decay_tables(seq, RET_PAD)
    whole = lambda shape: pl.BlockSpec(shape, lambda b: (0,) * len(shape))
    in_specs = [pl.BlockSpec((seq, width), lambda b: (b, 0)),
                pl.BlockSpec((1, 1, H, RET_DK, RET_DV), lambda b: (l, b, 0, 0, 0)),
                whole((H, RET_PAD, RET_PAD)), whole((H, RET_PAD, 1)), whole((H, RET_PAD, 1)), whole((H, 1, 1)),
                pl.BlockSpec((1, 1, H * RET_DV), lambda b: (l, 0, 0))]
    args = [qkvg, state, dec, qd, kd, gl, gn.reshape(gn.shape[0], 1, -1)]
    aliases = _chain(states, in_specs, args, 1)
    return pl.pallas_call(
        functools.partial(_ret_step_kernel, seq=seq),
        grid=(nb,),
        in_specs=in_specs,
        out_specs=[pl.BlockSpec((seq, H * RET_DV), lambda b: (b, 0)),
                   pl.BlockSpec((1, 1, H, RET_DK, RET_DV), lambda b: (l, b, 0, 0, 0))],
        out_shape=[jax.ShapeDtypeStruct((rows, H * RET_DV), F32),
                   jax.ShapeDtypeStruct(state.shape, F32)],
        input_output_aliases=aliases,
        compiler_params=_params(("arbitrary",)),
        name="retention_step",
    )(*args)


def _head_pairs(kcat, vcat):
    lo = lax.broadcasted_iota(jnp.int32, (kcat.shape[0], LANES), 1) < HEAD_DIM
    for pair in range(N_KV_HEADS // 2):
        kp = kcat[:, pair * LANES:(pair + 1) * LANES]
        vp = vcat[:, pair * LANES:(pair + 1) * LANES]
        kr = pltpu.roll(kp, HEAD_DIM, axis=1)
        vr = pltpu.roll(vp, HEAD_DIM, axis=1)
        for sub in range(2):
            ka, kb_ = (kp, kr) if sub == 0 else (kr, kp)
            va, vb_ = (vp, vr) if sub == 0 else (vr, vp)
            halves = [(jnp.where(lo, ka, 0.0).astype(BF16), jnp.where(lo, va, 0.0).astype(BF16)),
                      (jnp.where(lo, 0.0, kb_).astype(BF16), jnp.where(lo, 0.0, vb_).astype(BF16))]
            yield pair * 2 + sub, halves


def _sink_softmax(s, ok, sink):
    s = jnp.where(ok, s * (HEAD_DIM ** -0.5), NEG_INF)
    m = jnp.maximum(jnp.max(s, axis=-1, keepdims=True), sink)
    p = jnp.exp(s - m)
    return p * (1.0 / (jnp.sum(p, axis=-1, keepdims=True) + jnp.exp(sink - m)))


def _attend(q_of, kcat, vcat, ok, sink_ref, sink0, rows, store):
    pieces = GQA_GROUPS // 2
    for kh, halves in _head_pairs(kcat, vcat):
        base = kh * GQA_GROUPS * HEAD_DIM
        qs = jnp.concatenate([q_of(base + g * LANES) for g in range(pieces)], axis=0).astype(BF16)
        out = None
        for half, (kk, vv) in enumerate(halves):
            s = _bdot_nt(qs, kk)
            p = jnp.concatenate(
                [_sink_softmax(s[g * rows:(g + 1) * rows], ok, sink_ref[sink0 + kh * GQA_GROUPS + 2 * g + half])
                 for g in range(pieces)], axis=0)
            o = _bdot(p, vv)
            out = o if out is None else out + o
        for g in range(pieces):
            store(base + g * LANES, out[g * rows:(g + 1) * rows])


def _swa_kernel(sink_ref, q_ref, kp_ref, kc_ref, vp_ref, vc_ref, o_ref, *, sink0):
    i = pl.program_id(1)
    kcat = jnp.concatenate([kp_ref[0], kc_ref[0]], axis=0)
    vcat = jnp.concatenate([vp_ref[0], vc_ref[0]], axis=0)
    ql = lax.broadcasted_iota(jnp.int32, (ATT_BLOCK, 2 * ATT_BLOCK), 0)
    km = lax.broadcasted_iota(jnp.int32, (ATT_BLOCK, 2 * ATT_BLOCK), 1)
    no_prev = jnp.where(i > 0, 0, 4 * ATT_BLOCK)
    ok = ((km < ATT_BLOCK) & (km >= ql + no_prev)) | ((km >= ATT_BLOCK) & (km - ATT_BLOCK <= ql))

    def store(col0, val):
        o_ref[0, :, col0:col0 + LANES] = val.astype(o_ref.dtype)

    _attend(lambda c0: q_ref[0, :, c0:c0 + LANES], kcat, vcat, ok, sink_ref, sink0, ATT_BLOCK, store)


def swa_prompt(q, kv, sinks, j):
    B, T, DQ = q.shape
    DKV = kv.shape[2] // 2
    cur = lambda b, i: (b, i, 0)
    return pl.pallas_call(
        functools.partial(_swa_kernel, sink0=j * N_Q_HEADS),
        grid=(B, T // ATT_BLOCK),
        in_specs=[pl.BlockSpec(memory_space=pltpu.SMEM),
                  pl.BlockSpec((1, ATT_BLOCK, DQ), cur),
                  pl.BlockSpec((1, ATT_BLOCK, DKV), lambda b, i: (b, jnp.maximum(i - 1, 0), 0)),
                  pl.BlockSpec((1, ATT_BLOCK, DKV), lambda b, i: (b, i, 0)),
                  pl.BlockSpec((1, ATT_BLOCK, DKV), lambda b, i: (b, jnp.maximum(i - 1, 0), 1)),
                  pl.BlockSpec((1, ATT_BLOCK, DKV), lambda b, i: (b, i, 1))],
        out_specs=pl.BlockSpec((1, ATT_BLOCK, DQ), cur),
        out_shape=jax.ShapeDtypeStruct((B, T, DQ), BF16),
        compiler_params=_params(("arbitrary", "arbitrary")),
        name="swa_prompt",
    )(sinks, q, kv, kv, kv, kv)


def _swa_step_kernel(sink_ref, q_ref, ck_ref, nk_ref, cv_ref, nv_ref, o_ref, *, seq, wb, sink0):
    zpad = jnp.zeros((2 * ATT_BLOCK - wb - seq, nk_ref.shape[1]), F32)
    kcat = jnp.concatenate([ck_ref[0], nk_ref[...], zpad], axis=0)
    vcat = jnp.concatenate([cv_ref[0], nv_ref[...], zpad], axis=0)
    nk = kcat.shape[0]
    t = lax.broadcasted_iota(jnp.int32, (seq, nk), 0)
    s = lax.broadcasted_iota(jnp.int32, (seq, nk), 1)
    rel = t + wb - s
    ok = (rel >= 0) & (rel <= WINDOW)

    def store(col0, val):
        o_ref[:, col0:col0 + LANES] = val

    _attend(lambda c0: q_ref[:, c0:c0 + LANES], kcat, vcat, ok, sink_ref, sink0, seq, store)


def swa_step(q, cache_k, cache_v, kv_new, sinks, j, *, seq):
    rows, DQ = q.shape
    nb, wb, DKV = cache_k.shape
    return pl.pallas_call(
        functools.partial(_swa_step_kernel, seq=seq, wb=wb, sink0=j * N_Q_HEADS),
        grid=(nb,),
        in_specs=[pl.BlockSpec(memory_space=pltpu.SMEM),
                  pl.BlockSpec((seq, DQ), lambda b: (b, 0)),
                  pl.BlockSpec((1, wb, DKV), lambda b: (b, 0, 0)), pl.BlockSpec((seq, DKV), lambda b: (b, 0)),
                  pl.BlockSpec((1, wb, DKV), lambda b: (b, 0, 0)), pl.BlockSpec((seq, DKV), lambda b: (b, 1))],
        out_specs=pl.BlockSpec((seq, DQ), lambda b: (b, 0)),
        out_shape=jax.ShapeDtypeStruct((rows, DQ), F32),
        compiler_params=_params(("arbitrary",)),
        name="swa_step",
    )(sinks, q, cache_k, kv_new, cache_v, kv_new)


def _rope_tables_partial(pos):
    half = ROPE_DIM // 2
    inv = ROPE_THETA ** (-jnp.arange(half, dtype=F32) * 2.0 / ROPE_DIM)
    ang = pos.astype(F32)[:, None] * inv[None, :]
    cos, sin = jnp.cos(ang), jnp.sin(ang)
    T = pos.shape[0]
    ones = jnp.ones((T, HEAD_DIM - ROPE_DIM), F32)
    zeros = jnp.zeros((T, HEAD_DIM - ROPE_DIM), F32)
    zh = jnp.zeros((T, half), F32)
    c = jnp.concatenate([cos, cos, ones], axis=1)
    s1 = jnp.concatenate([-sin, zh, zeros], axis=1)
    s2 = jnp.concatenate([zh, sin, zeros], axis=1)
    rep = LANES // HEAD_DIM
    return tuple(jnp.tile(a, (1, rep)) for a in (c, s1, s2))


def _final_norm_kernel(x_ref, g_ref, o_ref):
    x = x_ref[0]
    ms = jnp.mean(x * x, axis=-1, keepdims=True)
    o_ref[0] = x * lax.rsqrt(ms + EPS) * g_ref[...]


def final_norm(x, g, tm):
    B, T, D = x.shape
    return pl.pallas_call(
        _final_norm_kernel,
        grid=(B, T // tm),
        in_specs=[pl.BlockSpec((1, tm, D), lambda b, i: (b, i, 0)), pl.BlockSpec((1, D), lambda b, i: (0, 0))],
        out_specs=pl.BlockSpec((1, tm, D), lambda b, i: (b, i, 0)),
        out_shape=jax.ShapeDtypeStruct((B, T, D), F32),
        compiler_params=_params(("arbitrary", "arbitrary")),
        name="final_norm",
    )(x, g.reshape(1, D))


def kernel(x_prompt, x_sample, state_ret, cache_win_k, cache_win_v, state_conv, c_prompt, c_sample,
           w_ada, b_ada, norm_mix, norm_ffn, ret_w_in, ret_gn, ret_w_out,
           kv_norm, kv_w_ada, kv_b_ada, w_kv, att_w_q, att_sinks, att_w_o,
           ffn_w_up, ffn_conv_w, ffn_conv_b, ffn_w_down, norm_f):
    D = D_MODEL
    BP, TP, _ = x_prompt.shape
    BS, TS, _ = x_sample.shape
    RS = BS * TS
    KV = N_KV_HEADS * HEAD_DIM
    QK = RET_HEADS * RET_DK

    c_all = jnp.concatenate([c_sample, c_prompt], axis=0)
    c_all = jnp.pad(c_all, ((0, -c_all.shape[0] % PACK), (0, 0)))
    mods = Mods(ada_mods(c_all, w_ada, b_ada), BS)
    kv_mods = Mods(ada_mods(c_all, kv_w_ada[None], kv_b_ada[None]), BS)
    tile_col = lambda b, i, j: j

    pos_p = jnp.arange(TP, dtype=jnp.int32)
    pos_s = PAST_LEN + jnp.arange(TS, dtype=jnp.int32)
    tabs_p = _rope_tables_partial(pos_p)
    tabs_s = tuple(jnp.tile(a, (BS, 1)) for a in _rope_tables_partial(pos_s))
    full_p = _rope_tables_full(pos_p)
    full_s = tuple(jnp.tile(a, (BS, 1)) for a in _rope_tables_full(pos_s))
    sinks = att_sinks.reshape(-1)
    kv_norm1, w_kv1 = kv_norm[None], w_kv[None]

    x = x_sample.reshape(1, RS, D)
    wb = cache_win_k.shape[1]
    ck = cache_win_k.reshape(BS, wb, KV)
    cv = cache_win_v.reshape(BS, wb, KV)
    ret_s = None
    conv_s = []
    w16 = [dict() for _ in range(DEPTH)]
    for l in range(DEPTH):
        if l == N_A:
            kv_s, w16_kv = proj(x, kv_norm1, 0, kv_mods.sample(0, 0, D), kv_mods.sample(0, 1, D), w_kv1, 0, tm=RS,
                                tn=2 * KV, tok=TS, out_dtype=F32, rope="partial", rope_tabs=tabs_s, rope_cols=KV,
                                emit_w=True)
            kv_s = kv_s[0]
        sh1, sc1, sh2, sc2 = (mods.sample(l, k, D) for k in (0, 1, 3, 4))
        if l < N_A:
            qkvg, w16[l]["in"] = proj(x, norm_mix, l, sh1, sc1, ret_w_in, l, tm=RS, tn=1024, tok=TS, out_dtype=F32,
                                      rope="full", rope_tabs=full_s, rope_cols=2 * QK, scale_from=QK, emit_w=True)
            a, ret_s = retention_step(qkvg[0], state_ret, ret_gn, l, ret_s, seq=TS)
            x, w16[l]["out"] = out_proj(a[None], ret_w_out, l, x, mods.sample(l, 2, 512, tile_col), tm=RS, tn=512,
                                        tok=TS, emit_w=True)
        else:
            j = l - N_A
            q, w16[l]["in"] = proj(x, norm_mix, l, sh1, sc1, att_w_q, j, tm=RS, tn=1024, tok=TS, out_dtype=F32,
                                   rope="partial", rope_tabs=tabs_s, rope_cols=D, emit_w=True)
            a = swa_step(q[0], ck, cv, kv_s, sinks, j, seq=TS)
            x, w16[l]["out"] = out_proj(a[None], att_w_o, j, x, mods.sample(l, 2, 1024, tile_col), tm=RS, tn=1024,
                                        tok=TS, emit_w=True)
        z, ua, ub, w16[l]["up_a"], w16[l]["up_b"] = upconv_step(x, norm_ffn, l, sh2, sc2, ffn_w_up, ffn_conv_w,
                                                                ffn_conv_b, state_conv, seq=TS)
        conv_s.append(jnp.concatenate([ua.reshape(BS, TS, D_FF)[:, TS - (CONV_W - 1):],
                                       ub.reshape(BS, TS, D_FF)[:, TS - (CONV_W - 1):]], axis=-1))
        x, w16[l]["down"] = out_proj(z[None], ffn_w_down, l, x, mods.sample(l, 5, 512, tile_col), tm=RS, tn=512,
                                     tok=TS, emit_w=True)
    y_sample = final_norm(x, norm_f, RS).reshape(BS, TS, D)

    x = x_prompt
    ret_p = None
    conv_p = []
    for l in range(DEPTH):
        if l == N_A:
            kv_p = proj(x, kv_norm1, 0, kv_mods.prompt(0, 0, D), kv_mods.prompt(0, 1, D), w16_kv, 0, tm=1024, tn=2 * KV,
                        tok=None, out_dtype=F32, rope="partial", rope_tabs=tabs_p, rope_cols=KV)
        sh1, sc1, sh2, sc2 = (mods.prompt(l, k, D) for k in (0, 1, 3, 4))
        if l < N_A:
            qkvg = proj(x, norm_mix, l, sh1, sc1, w16[l]["in"], 0, tm=1024, tn=1024, tok=None, out_dtype=F32,
                        rope="full", rope_tabs=full_p, rope_cols=2 * QK, scale_from=QK)
            a, ret_p = retention_prompt(qkvg, ret_gn, l, ret_p)
            x = out_proj(a, w16[l]["out"], 0, x, mods.prompt(l, 2, 512, tile_col), tm=1024, tn=512, tok=None)
        else:
            j = l - N_A
            q = proj(x, norm_mix, l, sh1, sc1, w16[l]["in"], 0, tm=1024, tn=1024, tok=None, out_dtype=BF16,
                     rope="partial", rope_tabs=tabs_p, rope_cols=D)
            a = swa_prompt(q, kv_p, sinks, j)
            x = out_proj(a, w16[l]["out"], 0, x, mods.prompt(l, 2, 1024, tile_col), tm=1024, tn=1024, tok=None)
        z, st = upconv_prompt(x, norm_ffn, l, sh2, sc2, w16[l]["up_a"], w16[l]["up_b"], ffn_conv_w, ffn_conv_b)
        conv_p.append(st[:, -1, :, SUBLANES - (CONV_W - 1):, :].transpose(0, 2, 1, 3).reshape(BP, CONV_W - 1, 2 * D_FF))
        x = out_proj(z, w16[l]["down"], 0, x, mods.prompt(l, 5, 512, tile_col), tm=1024, tn=512, tok=None)
    y_prompt = final_norm(x, norm_f, 512)
    wp = min(WINDOW, TP)
    win_k_prompt = kv_p[:, -wp:, :KV].reshape(BP, wp, N_KV_HEADS, HEAD_DIM)
    win_v_prompt = kv_p[:, -wp:, KV:].reshape(BP, wp, N_KV_HEADS, HEAD_DIM)
    k_all = jnp.concatenate([ck, kv_s[:, :KV].reshape(BS, TS, KV)], axis=1)
    v_all = jnp.concatenate([cv, kv_s[:, KV:].reshape(BS, TS, KV)], axis=1)
    win_k_sample = k_all[:, -wb:].reshape(BS, wb, N_KV_HEADS, HEAD_DIM)
    win_v_sample = v_all[:, -wb:].reshape(BS, wb, N_KV_HEADS, HEAD_DIM)

    return (y_prompt, y_sample, ret_p, ret_s, win_k_prompt, win_v_prompt,
            win_k_sample, win_v_sample, jnp.stack(conv_p), jnp.stack(conv_s))
```

```python
import functools

import jax
import jax.numpy as jnp
from jax import lax
from jax.experimental import pallas as pl
from jax.experimental.pallas import tpu as pltpu

F32 = jnp.float32
BF16 = jnp.bfloat16

D_MODEL = 2048
DEPTH = 4
PAST_LEN = 16384
N_A = DEPTH // 2
RET_HEADS = 8
RET_DK = D_MODEL // RET_HEADS
RET_DV = 2 * D_MODEL // RET_HEADS
RET_ROPE_BASE = 10000.0
N_Q_HEADS = 32
N_KV_HEADS = 4
HEAD_DIM = D_MODEL // N_Q_HEADS
GQA_GROUPS = N_Q_HEADS // N_KV_HEADS
ROPE_DIM = HEAD_DIM // 4
ROPE_THETA = 500000.0
WINDOW = 128
ATT_BLOCK = 128
D_FF = 2 * D_MODEL
CONV_W = 3
N_MOD = 6
EPS = 1e-6
NEG_INF = -1e30

LANES = 128
SUBLANES = 8
PACK = 16
HALO = PACK
RET_BLOCK = 256
RET_HEADS_PER_STEP = 2
RET_PAD = PACK
VMEM_LIMIT = 56 * 1024 * 1024


def _params(sem):
    return pltpu.CompilerParams(dimension_semantics=sem, vmem_limit_bytes=VMEM_LIMIT)


def _silu(x):
    return x * jax.nn.sigmoid(x)


def _bdot(a, b):
    return jnp.dot(a.astype(BF16), b.astype(BF16), preferred_element_type=F32)


def _bdot_nt(a, b):
    return lax.dot_general(a.astype(BF16), b.astype(BF16), (((1,), (1,)), ((), ())), preferred_element_type=F32)


def _norm_mod(x, g, sc, sh):
    ms = jnp.mean(x * x, axis=-1, keepdims=True)
    y = x * lax.rsqrt(ms + EPS) * g
    return y * (1.0 + sc) + sh


class Mods:
    def __init__(self, arr, n_sample):
        self.a3 = arr
        self.a4 = arr.reshape(arr.shape[0], arr.shape[1], 1, arr.shape[2])
        self.n_sample = n_sample

    def prompt(self, l, k, width, col=lambda *g: 0, batch=lambda *g: g[0]):
        per = D_MODEL // width
        off = self.n_sample
        return self.a4, pl.BlockSpec((1, 1, 1, width), lambda *g: (l, off + batch(*g), 0, k * per + col(*g)))

    def sample(self, l, k, width, col=lambda *g: 0):
        per = D_MODEL // width
        return self.a3, pl.BlockSpec((1, self.n_sample, width), lambda *g: (l, 0, k * per + col(*g)))


def _ada_kernel(c_ref, w_ref, b_ref, o_ref):
    o_ref[0] = _bdot(_silu(c_ref[...]), w_ref[0]) + b_ref[0]


def ada_mods(c_all, w, b, tn=1024):
    L, D, N = w.shape
    R = c_all.shape[0]
    return pl.pallas_call(
        _ada_kernel,
        grid=(L, N // tn),
        in_specs=[pl.BlockSpec((R, D), lambda l, j: (0, 0)),
                  pl.BlockSpec((1, D, tn), lambda l, j: (l, 0, j)),
                  pl.BlockSpec((1, 1, tn), lambda l, j: (l, 0, j))],
        out_specs=pl.BlockSpec((1, R, tn), lambda l, j: (l, 0, j)),
        out_shape=jax.ShapeDtypeStruct((L, R, N), F32),
        compiler_params=_params(("arbitrary", "arbitrary")),
        name="ada_mods",
    )(c_all, w, b.reshape(L, 1, N))


def _store_normed(h_scr, r_scr, row0, x_ref, g_ref, sc_ref, sh_ref, rows, tok):
    def stats(r, carry):
        r0 = pl.multiple_of(r * PACK, PACK)
        x = x_ref[0, pl.ds(r0, PACK), :]
        r_scr[pl.ds(r0, PACK), :] = lax.rsqrt(jnp.mean(x * x, axis=-1, keepdims=True) + EPS)
        return carry

    lax.fori_loop(0, rows // PACK, stats, 0, unroll=4)
    g = g_ref[0]

    def piece(start, n, sc, sh):
        return x_ref[0, pl.ds(start, n), :] * r_scr[pl.ds(start, n), :] * g * (1.0 + sc) + sh

    def apply(r, carry):
        r0 = pl.multiple_of(r * PACK, PACK)
        if tok is None:
            h = piece(r0, PACK, sc_ref[0, 0], sh_ref[0, 0])
        else:
            parts = []
            for s in range(PACK // tok):
                m = r * (PACK // tok) + s
                start = pl.multiple_of(r0 + s * tok, tok)
                parts.append(piece(start, tok, sc_ref[0, pl.ds(m, 1), :], sh_ref[0, pl.ds(m, 1), :]))
            h = jnp.concatenate(parts, axis=0)
        h_scr[pl.ds(pl.multiple_of(row0 + r0, PACK), PACK), :] = h.astype(BF16)
        return carry

    lax.fori_loop(0, rows // PACK, apply, 0, unroll=2)


def _rope64(a, c, s1, s2):
    return a * c + pltpu.roll(a, LANES - ROPE_DIM // 2, axis=1) * s1 + pltpu.roll(a, ROPE_DIM // 2, axis=1) * s2


def _proj_kernel(*refs, tm, tn, tok, rope, rope_cols, scale_from, emit_w):
    x_ref, g_ref, sh_ref, sc_ref, w_ref = refs[:5]
    if rope:
        t1_ref, t2_ref, t3_ref = refs[5:8]
    o_ref = refs[-4] if emit_w else refs[-3]
    h_scr, r_scr = refs[-2:]
    j = pl.program_id(2)

    @pl.when(j == 0)
    def _():
        _store_normed(h_scr, r_scr, 0, x_ref, g_ref, sc_ref, sh_ref, tm, tok)

    wb = w_ref[0].astype(BF16)
    if emit_w:
        refs[-3][0] = wb
    acc = jnp.dot(h_scr[...], wb, preferred_element_type=F32)
    _proj_epilogue(acc, o_ref, j, tn, rope, rope_cols, scale_from, refs[5:8])


def _proj_epilogue(acc, o_ref, j, tn, rope, rope_cols, scale_from, tabs):
    def plain():
        o_ref[0] = acc.astype(o_ref.dtype)

    if rope is None:
        plain()
    elif rope == "partial":
        t1_ref, t2_ref, t3_ref = tabs
        for c in range(tn // LANES):
            a = acc[:, c * LANES:(c + 1) * LANES]
            if c * LANES < rope_cols:
                a = _rope64(a, t1_ref[...], t2_ref[...], t3_ref[...])
            o_ref[0, :, c * LANES:(c + 1) * LANES] = a.astype(o_ref.dtype)
    else:
        t1_ref, t2_ref, _ = tabs

        @pl.when(j * tn < rope_cols)
        def _():
            cos, sin = t1_ref[...], t2_ref[...]
            scale = jnp.where(j * tn >= scale_from, RET_DK ** -0.5, 1.0).astype(F32)
            for hd in range(tn // RET_DK):
                x1 = acc[:, hd * RET_DK:hd * RET_DK + LANES]
                x2 = acc[:, hd * RET_DK + LANES:(hd + 1) * RET_DK]
                o_ref[0, :, hd * RET_DK:hd * RET_DK + LANES] = ((x1 * cos - x2 * sin) * scale).astype(o_ref.dtype)
                o_ref[0, :, hd * RET_DK + LANES:(hd + 1) * RET_DK] = ((x2 * cos + x1 * sin) * scale).astype(o_ref.dtype)

        pl.when(j * tn >= rope_cols)(plain)


def _norm_rows(h_scr, slot, row0, x_ref, g_ref, sc_ref, sh_ref, start, n):
    g = g_ref[0]
    sc1 = 1.0 + sc_ref[0, 0]
    sh = sh_ref[0, 0]
    for c in range(n // PACK):
        r0 = pl.multiple_of(start + c * PACK, PACK)
        x = x_ref[0, pl.ds(r0, PACK), :]
        r = lax.rsqrt(jnp.mean(x * x, axis=-1, keepdims=True) + EPS)
        h_scr[slot, pl.ds(pl.multiple_of(row0 + r0, PACK), PACK), :] = (x * r * g * sc1 + sh).astype(BF16)


def _lag_rows(tm, nj):
    return PACK * -(-tm // (PACK * nj))


def _proj_lag_kernel(*refs, tm, tn, nj, rope, rope_cols, scale_from):
    x_ref, g_ref, sh_ref, sc_ref, w_ref = refs[:5]
    o_ref, h_scr = refs[-2:]
    t = pl.program_id(0)
    j = pl.program_id(1)
    n = _lag_rows(tm, nj)
    start = jnp.minimum(j * n, tm - n)
    wslot = t % 2

    @pl.when(t == 0)
    def _():
        _norm_rows(h_scr, wslot, 0, x_ref, g_ref, sc_ref, sh_ref, start, n)

    @pl.when(t > 0)
    def _():
        _norm_rows(h_scr, wslot, 0, x_ref, g_ref, sc_ref, sh_ref, start, n)
        acc = jnp.dot(h_scr[1 - wslot], w_ref[0], preferred_element_type=F32)
        _proj_epilogue(acc, o_ref, j, tn, rope, rope_cols, scale_from, refs[5:8])


def proj_prompt(x, g, l, mods, ks, w, *, tm, tn, out_dtype, rope=None, rope_tabs=None, rope_cols=0, scale_from=0):
    B, T, D = x.shape
    N = w.shape[2]
    assert rope != "partial" or rope_cols == N or tn == N
    tpb = T // tm
    nt = B * tpb
    nj = N // tn
    cur = lambda t: jnp.minimum(t, nt - 1)
    prev = lambda t: jnp.maximum(t - 1, 0)
    colj = lambda t, j: jnp.where(t > 0, j, 0)
    ml, ksh, ksc = ks
    sh = mods.prompt(ml, ksh, D, batch=lambda t, j: cur(t) // tpb)
    sc = mods.prompt(ml, ksc, D, batch=lambda t, j: cur(t) // tpb)
    in_specs = [pl.BlockSpec((1, tm, D), lambda t, j: (cur(t) // tpb, cur(t) % tpb, 0)),
                pl.BlockSpec((1, 1, D), lambda t, j: (l, 0, 0)),
                sh[1], sc[1],
                pl.BlockSpec((1, D, tn), lambda t, j: (0, 0, colj(t, j)))]
    args = [x, g.reshape(g.shape[0], 1, D), sh[0], sc[0], w]
    if rope:
        tabs = list(rope_tabs) + ([rope_tabs[0]] if len(rope_tabs) == 2 else [])
        in_specs += [pl.BlockSpec((tm, LANES), lambda t, j: (prev(t) % tpb, 0))] * 3
        args += tabs
    return pl.pallas_call(
        functools.partial(_proj_lag_kernel, tm=tm, tn=tn, nj=nj, rope=rope, rope_cols=rope_cols, scale_from=scale_from),
        grid=(nt + 1, nj),
        in_specs=in_specs,
        out_specs=pl.BlockSpec((1, tm, tn), lambda t, j: (prev(t) // tpb, prev(t) % tpb, colj(t, j))),
        out_shape=jax.ShapeDtypeStruct((B, T, N), out_dtype),
        scratch_shapes=[pltpu.VMEM((2, tm, D), BF16)],
        compiler_params=_params(("arbitrary", "arbitrary")),
        name="proj_prompt",
    )(*args)


def proj(x, g, l, sh, sc, w, wl, *, tm, tn, tok, out_dtype, rope=None, rope_tabs=None, rope_cols=0, scale_from=0,
         emit_w=False):
    B, T, D = x.shape
    N = w.shape[2]
    assert rope != "partial" or rope_cols == N or tn == N
    assert not emit_w or (B == 1 and T == tm)
    out_specs = [pl.BlockSpec((1, tm, tn), lambda b, i, j: (b, i, j))]
    out_shape = [jax.ShapeDtypeStruct((B, T, N), out_dtype)]
    if emit_w:
        out_specs.append(pl.BlockSpec((1, D, tn), lambda b, i, j: (0, 0, j)))
        out_shape.append(jax.ShapeDtypeStruct((1, D, N), BF16))
    in_specs = [pl.BlockSpec((1, tm, D), lambda b, i, j: (b, i, 0)),
                pl.BlockSpec((1, 1, D), lambda b, i, j: (l, 0, 0)),
                sh[1], sc[1],
                pl.BlockSpec((1, D, tn), lambda b, i, j: (wl, 0, j))]
    args = [x, g.reshape(g.shape[0], 1, D), sh[0], sc[0], w]
    if rope:
        tabs = list(rope_tabs) + ([rope_tabs[0]] if len(rope_tabs) == 2 else [])
        in_specs += [pl.BlockSpec((tm, LANES), lambda b, i, j: (i, 0))] * 3
        args += tabs
    outs = pl.pallas_call(
        functools.partial(_proj_kernel, tm=tm, tn=tn, tok=tok, rope=rope, rope_cols=rope_cols, scale_from=scale_from,
                          emit_w=emit_w),
        grid=(B, T // tm, N // tn),
        in_specs=in_specs,
        out_specs=out_specs,
        out_shape=out_shape,
        scratch_shapes=[pltpu.VMEM((tm, D), BF16), pltpu.VMEM((tm, 1), F32)],
        compiler_params=_params(("arbitrary", "arbitrary", "arbitrary")),
        name="proj",
    )(*args)
    return outs if emit_w else outs[0]


def _conv_gate(ua, ub, cwa_ref, cwb_ref, cba_ref, cbb_ref):
    def conv(u, cw_ref, cb_ref):
        acc = cb_ref[0] + cw_ref[0, 0:1, :] * u[0]
        acc = acc + cw_ref[0, 1:2, :] * u[1]
        return acc + cw_ref[0, 2:3, :] * u[2]

    return _silu(conv(ua, cwa_ref, cba_ref)) * conv(ub, cwb_ref, cbb_ref)


def _upconv_kernel(x_ref, g_ref, sh_ref, sc_ref, wa_ref, wb_ref, cwa_ref, cwb_ref, cba_ref, cbb_ref,
                   z_ref, st_ref, h_scr, *, tm, nj, tpb, nt):
    t = pl.program_id(0)
    n = _lag_rows(tm, nj)
    start = jnp.minimum(pl.program_id(1) * n, tm - n)
    wslot = t % 2
    zero_halo = jnp.zeros((HALO, h_scr.shape[2]), BF16)

    @pl.when(t == 0)
    def _():
        _norm_rows(h_scr, wslot, HALO, x_ref, g_ref, sc_ref, sh_ref, start, n)
        h_scr[wslot, 0:HALO, :] = zero_halo

    @pl.when(t > 0)
    def _():
        _norm_rows(h_scr, wslot, HALO, x_ref, g_ref, sc_ref, sh_ref, start, n)
        tail = h_scr[1 - wslot, tm:tm + HALO, :]
        h_scr[wslot, 0:HALO, :] = jnp.where(jnp.minimum(t, nt - 1) % tpb > 0, tail, zero_halo)
        h = h_scr[1 - wslot]

        def taps(w_ref):
            u = jnp.dot(h, w_ref[0], preferred_element_type=F32)
            return (pltpu.roll(u, 2, axis=0)[HALO:], pltpu.roll(u, 1, axis=0)[HALO:], u[HALO:])

        ua = taps(wa_ref)
        ub = taps(wb_ref)
        z_ref[0] = _conv_gate(ua, ub, cwa_ref, cwb_ref, cba_ref, cbb_ref).astype(z_ref.dtype)
        st_ref[0, 0, 0] = ua[2][tm - SUBLANES:, :]
        st_ref[0, 0, 1] = ub[2][tm - SUBLANES:, :]


def _halves_specs(shape, l, nj, col):
    return [pl.BlockSpec(shape, lambda *g: (l, 0, col(*g))), pl.BlockSpec(shape, lambda *g: (l, 0, col(*g) + nj))]


def upconv_prompt(x, g, l, mods, ks, wa, wb, conv_w, conv_b, *, tm=1024, tn=512):
    B, T, D = x.shape
    F = wa.shape[2]
    nj = F // tn
    L = conv_w.shape[0]
    tpb = T // tm
    nt = B * tpb
    cur = lambda t: jnp.minimum(t, nt - 1)
    prev = lambda t: jnp.maximum(t - 1, 0)
    colj = lambda t, j: jnp.where(t > 0, j, 0)
    ml, ksh, ksc = ks
    sh = mods.prompt(ml, ksh, D, batch=lambda t, j: cur(t) // tpb)
    sc = mods.prompt(ml, ksc, D, batch=lambda t, j: cur(t) // tpb)
    in_specs = [pl.BlockSpec((1, tm, D), lambda t, j: (cur(t) // tpb, cur(t) % tpb, 0)),
                pl.BlockSpec((1, 1, D), lambda t, j: (l, 0, 0)),
                sh[1], sc[1],
                pl.BlockSpec((1, D, tn), lambda t, j: (0, 0, colj(t, j))),
                pl.BlockSpec((1, D, tn), lambda t, j: (0, 0, colj(t, j)))]
    in_specs += _halves_specs((1, CONV_W, tn), l, nj, colj) + _halves_specs((1, 1, tn), l, nj, colj)
    cb = conv_b.reshape(L, 1, 2 * F)
    return pl.pallas_call(
        functools.partial(_upconv_kernel, tm=tm, nj=nj, tpb=tpb, nt=nt),
        grid=(nt + 1, nj),
        in_specs=in_specs,
        out_specs=[pl.BlockSpec((1, tm, tn), lambda t, j: (prev(t) // tpb, prev(t) % tpb, colj(t, j))),
                   pl.BlockSpec((1, 1, 2, SUBLANES, tn),
                                lambda t, j: (prev(t) // tpb, prev(t) % tpb, 0, 0, colj(t, j)))],
        out_shape=[jax.ShapeDtypeStruct((B, T, F), BF16),
                   jax.ShapeDtypeStruct((B, tpb, 2, SUBLANES, F), F32)],
        scratch_shapes=[pltpu.VMEM((2, HALO + tm, D), BF16)],
        compiler_params=_params(("arbitrary", "arbitrary")),
        name="upconv_prompt",
    )(x, g.reshape(L, 1, D), sh[0], sc[0], wa, wb, conv_w, conv_w, cb, cb)


def _upconv_step_kernel(x_ref, g_ref, sh_ref, sc_ref, wa_ref, wb_ref, cwa_ref, cwb_ref, cba_ref, cbb_ref,
                        sta_ref, stb_ref, z_ref, ua_ref, ub_ref, wao_ref, wbo_ref, h_scr, r_scr, *, rows, seq):
    @pl.when(pl.program_id(0) == 0)
    def _():
        _store_normed(h_scr, r_scr, 0, x_ref, g_ref, sc_ref, sh_ref, rows, seq)

    h = h_scr[...]
    nb = rows // seq

    def taps(w_ref, st_ref, u_ref, wo_ref):
        w = w_ref[0].astype(BF16)
        wo_ref[0] = w
        u = jnp.dot(h, w, preferred_element_type=F32)
        u_ref[...] = u
        u3 = u.reshape(nb, seq, u.shape[1])
        t = lax.broadcasted_iota(jnp.int32, u3.shape, 1)
        st0 = st_ref[0, :, 0:1, :]
        st1 = st_ref[0, :, 1:2, :]
        um1 = jnp.where(t == 0, st1, pltpu.roll(u3, 1, axis=1))
        um2 = jnp.where(t == 0, st0, jnp.where(t == 1, st1, pltpu.roll(u3, 2, axis=1)))
        return (um2, um1, u3)

    ua = taps(wa_ref, sta_ref, ua_ref, wao_ref)
    ub = taps(wb_ref, stb_ref, ub_ref, wbo_ref)
    z = _conv_gate(ua, ub, cwa_ref, cwb_ref, cba_ref, cbb_ref)
    z_ref[...] = z.reshape(rows, z.shape[2]).astype(z_ref.dtype)


def upconv_step(x, g, l, sh, sc, w_up, conv_w, conv_b, state, *, seq, tn=512):
    _, rows, D = x.shape
    F = w_up.shape[2] // 2
    nj = F // tn
    nb = rows // seq
    L = w_up.shape[0]
    col = lambda j: j
    in_specs = [pl.BlockSpec((1, rows, D), lambda j: (0, 0, 0)),
                pl.BlockSpec((1, 1, D), lambda j: (l, 0, 0)),
                sh[1], sc[1]]
    in_specs += _halves_specs((1, D, tn), l, nj, col) + _halves_specs((1, CONV_W, tn), l, nj, col)
    in_specs += _halves_specs((1, 1, tn), l, nj, col)
    in_specs += [pl.BlockSpec((1, nb, CONV_W - 1, tn), lambda j: (l, 0, 0, j)),
                 pl.BlockSpec((1, nb, CONV_W - 1, tn), lambda j: (l, 0, 0, j + nj))]
    cb = conv_b.reshape(L, 1, 2 * F)
    return pl.pallas_call(
        functools.partial(_upconv_step_kernel, rows=rows, seq=seq),
        grid=(nj,),
        in_specs=in_specs,
        out_specs=[pl.BlockSpec((rows, tn), lambda j: (0, j))] * 3 + [pl.BlockSpec((1, D, tn), lambda j: (0, 0, j))] * 2,
        out_shape=[jax.ShapeDtypeStruct((rows, F), BF16), jax.ShapeDtypeStruct((rows, F), F32),
                   jax.ShapeDtypeStruct((rows, F), F32), jax.ShapeDtypeStruct((1, D, F), BF16),
                   jax.ShapeDtypeStruct((1, D, F), BF16)],
        scratch_shapes=[pltpu.VMEM((rows, D), BF16), pltpu.VMEM((rows, 1), F32)],
        compiler_params=_params(("arbitrary",)),
        name="upconv_step",
    )(x, g.reshape(L, 1, D), sh[0], sc[0], w_up, w_up, conv_w, conv_w, cb, cb, state, state)


def _out_kernel(a_ref, w_ref, x_ref, gt_ref, o_ref, *wo_ref, tok):
    w = w_ref[0].astype(BF16)
    if wo_ref:
        wo_ref[0][0] = w
    y = jnp.dot(a_ref[0].astype(BF16), w, preferred_element_type=F32)
    if tok is None:
        o_ref[0] = x_ref[0] + gt_ref[0, 0] * y
    else:
        for b in range(y.shape[0] // tok):
            rows = slice(b * tok, (b + 1) * tok)
            o_ref[0, rows, :] = x_ref[0, rows, :] + gt_ref[0, b:b + 1, :] * y[rows]


def out_proj(a, w, l, x, gate, *, tm, tn, tok, emit_w=False):
    B, T, K = a.shape
    N = w.shape[2]
    assert not emit_w or (B == 1 and T == tm)
    out_specs = [pl.BlockSpec((1, tm, tn), lambda b, i, j: (b, i, j))]
    out_shape = [jax.ShapeDtypeStruct((B, T, N), F32)]
    if emit_w:
        out_specs.append(pl.BlockSpec((1, K, tn), lambda b, i, j: (0, 0, j)))
        out_shape.append(jax.ShapeDtypeStruct((1, K, N), BF16))
    outs = pl.pallas_call(
        functools.partial(_out_kernel, tok=tok),
        grid=(B, T // tm, N // tn),
        in_specs=[pl.BlockSpec((1, tm, K), lambda b, i, j: (b, i, 0)),
                  pl.BlockSpec((1, K, tn), lambda b, i, j: (l, 0, j)),
                  pl.BlockSpec((1, tm, tn), lambda b, i, j: (b, i, j)),
                  gate[1]],
        out_specs=out_specs,
        out_shape=out_shape,
        compiler_params=_params(("arbitrary", "arbitrary", "arbitrary")),
        name="out_proj",
    )(a, w, x, gate[0])
    return outs if emit_w else outs[0]


def _rope_full(x, cos, sin):
    x1 = x[:, :LANES]
    x2 = x[:, LANES:]
    return jnp.concatenate([x1 * cos - x2 * sin, x2 * cos + x1 * sin], axis=1)


def _retention_head(q, k, v, gate, S, dec, qd, kd, gl, gn):
    qb = q.astype(BF16)
    vb = v.astype(BF16)
    scores = _bdot_nt(qb, k) * dec
    o = _bdot(scores, vb) + _bdot(qb, S) * qd
    kdt = (k * kd).T
    s_new = gl * S + _bdot(kdt, vb)
    mu = jnp.mean(o, axis=-1, keepdims=True)
    d = o - mu
    var = jnp.mean(d * d, axis=-1, keepdims=True)
    on = d * lax.rsqrt(var + EPS) * gn
    return _silu(gate) * on, s_new


def _ret_kernel(q_ref, k_ref, v_ref, g_ref, dec_ref, qd_ref, kd_ref, gl_ref, gn_ref, *rest, lsel):
    o_ref, s_ref = rest[-2:]

    @pl.when(pl.program_id(2) == 0)
    def _():
        s_ref[...] = jnp.zeros_like(s_ref)

    for hd in range(RET_HEADS_PER_STEP):
        kc = slice(hd * RET_DK, (hd + 1) * RET_DK)
        vc = slice(hd * RET_DV, (hd + 1) * RET_DV)
        o, s_new = _retention_head(q_ref[0, :, kc], k_ref[0, :, kc], v_ref[0, :, vc], g_ref[0, :, vc],
                                   s_ref[lsel, 0, hd], dec_ref[hd], qd_ref[hd], kd_ref[hd], gl_ref[hd],
                                   gn_ref[0, :, vc])
        s_ref[lsel, 0, hd] = s_new
        o_ref[0, :, vc] = o.astype(o_ref.dtype)


def _decay_tables(L, Lp):
    log_gamma = jnp.log(1.0 - jnp.exp2(-5.0 - jnp.arange(RET_HEADS, dtype=F32)))
    idx = jnp.arange(L, dtype=F32)
    rel = idx[:, None] - idx[None, :]
    dec = jnp.where(rel >= 0, jnp.exp(jnp.maximum(rel, 0.0)[None] * log_gamma[:, None, None]), 0.0)
    qd = jnp.exp((idx + 1.0)[None, :] * log_gamma[:, None])[..., None]
    kd = jnp.exp((L - 1.0 - idx)[None, :] * log_gamma[:, None])[..., None]
    gl = jnp.exp(L * log_gamma).reshape(RET_HEADS, 1, 1)
    p = Lp - L
    return (jnp.pad(dec, ((0, 0), (0, p), (0, p))), jnp.pad(qd, ((0, 0), (0, p), (0, 0))),
            jnp.pad(kd, ((0, 0), (0, p), (0, 0))), gl)


def _rope_tables_full(pos):
    half = RET_DK // 2
    inv = RET_ROPE_BASE ** (-jnp.arange(half, dtype=F32) * 2.0 / RET_DK)
    ang = pos.astype(F32)[:, None] * inv[None, :]
    return jnp.cos(ang), jnp.sin(ang)


def _chain(prev, in_specs, args, out_index):
    if prev is None:
        return {}
    in_specs.append(pl.BlockSpec(memory_space=pl.ANY))
    args.append(prev)
    return {len(args) - 1: out_index}


def _layer_block(prev, l):
    return (N_A, 0, l) if prev is None else (1, l, 0)


def retention_prompt(qkvg, gn, l, states):
    B, T, _ = qkvg.shape
    L = RET_BLOCK
    H = RET_HEADS
    P = RET_HEADS_PER_STEP
    dec, qd, kd, gl = _decay_tables(L, L)
    qk_blocks = H // P
    v0 = 2 * H * RET_DK // (P * RET_DV)
    in_specs = [pl.BlockSpec((1, L, P * RET_DK), lambda b, h, c: (b, c, h)),
                pl.BlockSpec((1, L, P * RET_DK), lambda b, h, c: (b, c, qk_blocks + h)),
                pl.BlockSpec((1, L, P * RET_DV), lambda b, h, c: (b, c, v0 + h)),
                pl.BlockSpec((1, L, P * RET_DV), lambda b, h, c: (b, c, v0 + qk_blocks + h)),
                pl.BlockSpec((P, L, L), lambda b, h, c: (h, 0, 0)),
                pl.BlockSpec((P, L, 1), lambda b, h, c: (h, 0, 0)),
                pl.BlockSpec((P, L, 1), lambda b, h, c: (h, 0, 0)),
                pl.BlockSpec((P, 1, 1), lambda b, h, c: (h, 0, 0)),
                pl.BlockSpec((1, 1, P * RET_DV), lambda b, h, c: (l, 0, h))]
    args = [qkvg, qkvg, qkvg, qkvg, dec, qd, kd, gl, gn.reshape(gn.shape[0], 1, -1)]
    aliases = _chain(states, in_specs, args, 1)
    nl, lb, lsel = _layer_block(states, l)
    return pl.pallas_call(
        functools.partial(_ret_kernel, lsel=lsel),
        grid=(B, H // P, T // L),
        in_specs=in_specs,
        out_specs=[pl.BlockSpec((1, L, P * RET_DV), lambda b, h, c: (b, c, h)),
                   pl.BlockSpec((nl, 1, P, RET_DK, RET_DV), lambda b, h, c: (lb, b, h, 0, 0))],
        out_shape=[jax.ShapeDtypeStruct((B, T, H * RET_DV), BF16),
                   jax.ShapeDtypeStruct((N_A, B, H, RET_DK, RET_DV), F32)],
        input_output_aliases=aliases,
        compiler_params=_params(("arbitrary", "arbitrary", "arbitrary")),
        name="retention_prompt",
    )(*args)


def _ret_step_kernel(x_ref, s_ref, dec_ref, qd_ref, kd_ref, gl_ref, gn_ref, *rest, seq, lsel):
    o_ref, so_ref = rest[-2:]
    H = RET_HEADS
    zpad = jnp.zeros((RET_PAD - seq, RET_DV), F32)
    for other in range(so_ref.shape[0]):
        if other != lsel:
            so_ref[other] = jnp.zeros(so_ref.shape[1:], F32)

    def padded(col0, width):
        return jnp.concatenate([x_ref[:, col0:col0 + width], zpad[:, :width]], axis=0)

    for h in range(H):
        q = padded(h * RET_DK, RET_DK)
        k = padded(H * RET_DK + h * RET_DK, RET_DK)
        v = padded(2 * H * RET_DK + h * RET_DV, RET_DV)
        gate = padded(2 * H * RET_DK + H * RET_DV + h * RET_DV, RET_DV)
        o, s_new = _retention_head(q, k, v, gate, s_ref[0, 0, h], dec_ref[h], qd_ref[h], kd_ref[h],
                                   gl_ref[h], gn_ref[0, :, h * RET_DV:(h + 1) * RET_DV])
        so_ref[lsel, 0, h] = s_new
        o_ref[:, h * RET_DV:(h + 1) * RET_DV] = o[:seq]


def retention_step(qkvg, state, gn, l, states, *, seq):
    rows, width = qkvg.shape
    nb = rows // seq
    H = RET_HEADS
    dec, qd, kd, gl = _decay_tables(seq, RET_PAD)
    whole = lambda shape: pl.BlockSpec(shape, lambda b: (0,) * len(shape))
    in_specs = [pl.BlockSpec((seq, width), lambda b: (b, 0)),
                pl.BlockSpec((1, 1, H, RET_DK, RET_DV), lambda b: (l, b, 0, 0, 0)),
                whole((H, RET_PAD, RET_PAD)), whole((H, RET_PAD, 1)), whole((H, RET_PAD, 1)), whole((H, 1, 1)),
                pl.BlockSpec((1, 1, H * RET_DV), lambda b: (l, 0, 0))]
    args = [qkvg, state, dec, qd, kd, gl, gn.reshape(gn.shape[0], 1, -1)]
    aliases = _chain(states, in_specs, args, 1)
    nl, lb, lsel = _layer_block(states, l)
    return pl.pallas_call(
        functools.partial(_ret_step_kernel, seq=seq, lsel=lsel),
        grid=(nb,),
        in_specs=in_specs,
        out_specs=[pl.BlockSpec((seq, H * RET_DV), lambda b: (b, 0)),
                   pl.BlockSpec((nl, 1, H, RET_DK, RET_DV), lambda b: (lb, b, 0, 0, 0))],
        out_shape=[jax.ShapeDtypeStruct((rows, H * RET_DV), F32),
                   jax.ShapeDtypeStruct(state.shape, F32)],
        input_output_aliases=aliases,
        compiler_params=_params(("arbitrary",)),
        name="retention_step",
    )(*args)


def _head_pairs(kcat, vcat):
    lo = lax.broadcasted_iota(jnp.int32, (kcat.shape[0], LANES), 1) < HEAD_DIM
    for pair in range(N_KV_HEADS // 2):
        kp = kcat[:, pair * LANES:(pair + 1) * LANES]
        vp = vcat[:, pair * LANES:(pair + 1) * LANES]
        kr = pltpu.roll(kp, HEAD_DIM, axis=1)
        vr = pltpu.roll(vp, HEAD_DIM, axis=1)
        for sub in range(2):
            ka, kb_ = (kp, kr) if sub == 0 else (kr, kp)
            va, vb_ = (vp, vr) if sub == 0 else (vr, vp)
            halves = [(jnp.where(lo, ka, 0.0).astype(BF16), jnp.where(lo, va, 0.0).astype(BF16)),
                      (jnp.where(lo, 0.0, kb_).astype(BF16), jnp.where(lo, 0.0, vb_).astype(BF16))]
            yield pair * 2 + sub, halves


def _sink_softmax(s, ok, sink):
    s = jnp.where(ok, s * (HEAD_DIM ** -0.5), NEG_INF)
    m = jnp.maximum(jnp.max(s, axis=-1, keepdims=True), sink)
    p = jnp.exp(s - m)
    return p * (1.0 / (jnp.sum(p, axis=-1, keepdims=True) + jnp.exp(sink - m)))


def _attend(q_of, kcat, vcat, ok, sink_ref, sink0, rows, store):
    pieces = GQA_GROUPS // 2
    for kh, halves in _head_pairs(kcat, vcat):
        base = kh * GQA_GROUPS * HEAD_DIM
        qs = jnp.concatenate([q_of(base + g * LANES) for g in range(pieces)], axis=0).astype(BF16)
        out = None
        for half, (kk, vv) in enumerate(halves):
            s = _bdot_nt(qs, kk)
            p = jnp.concatenate(
                [_sink_softmax(s[g * rows:(g + 1) * rows], ok, sink_ref[sink0 + kh * GQA_GROUPS + 2 * g + half])
                 for g in range(pieces)], axis=0)
            o = _bdot(p, vv)
            out = o if out is None else out + o
        for g in range(pieces):
            store(base + g * LANES, out[g * rows:(g + 1) * rows])


def _swa_kernel(sink_ref, q_ref, kp_ref, kc_ref, vp_ref, vc_ref, o_ref, *, sink0):
    i = pl.program_id(1)
    kcat = jnp.concatenate([kp_ref[0], kc_ref[0]], axis=0)
    vcat = jnp.concatenate([vp_ref[0], vc_ref[0]], axis=0)
    ql = lax.broadcasted_iota(jnp.int32, (ATT_BLOCK, 2 * ATT_BLOCK), 0)
    km = lax.broadcasted_iota(jnp.int32, (ATT_BLOCK, 2 * ATT_BLOCK), 1)
    no_prev = jnp.where(i > 0, 0, 4 * ATT_BLOCK)
    ok = ((km < ATT_BLOCK) & (km >= ql + no_prev)) | ((km >= ATT_BLOCK) & (km - ATT_BLOCK <= ql))

    def store(col0, val):
        o_ref[0, :, col0:col0 + LANES] = val.astype(o_ref.dtype)

    _attend(lambda c0: q_ref[0, :, c0:c0 + LANES], kcat, vcat, ok, sink_ref, sink0, ATT_BLOCK, store)


def swa_prompt(q, kv, sinks, j):
    B, T, DQ = q.shape
    DKV = kv.shape[2] // 2
    cur = lambda b, i: (b, i, 0)
    return pl.pallas_call(
        functools.partial(_swa_kernel, sink0=j * N_Q_HEADS),
        grid=(B, T // ATT_BLOCK),
        in_specs=[pl.BlockSpec(memory_space=pltpu.SMEM),
                  pl.BlockSpec((1, ATT_BLOCK, DQ), cur),
                  pl.BlockSpec((1, ATT_BLOCK, DKV), lambda b, i: (b, jnp.maximum(i - 1, 0), 0)),
                  pl.BlockSpec((1, ATT_BLOCK, DKV), lambda b, i: (b, i, 0)),
                  pl.BlockSpec((1, ATT_BLOCK, DKV), lambda b, i: (b, jnp.maximum(i - 1, 0), 1)),
                  pl.BlockSpec((1, ATT_BLOCK, DKV), lambda b, i: (b, i, 1))],
        out_specs=pl.BlockSpec((1, ATT_BLOCK, DQ), cur),
        out_shape=jax.ShapeDtypeStruct((B, T, DQ), BF16),
        compiler_params=_params(("arbitrary", "arbitrary")),
        name="swa_prompt",
    )(sinks, q, kv, kv, kv, kv)


def _swa_step_kernel(sink_ref, q_ref, ck_ref, nk_ref, cv_ref, nv_ref, o_ref, *, seq, wb, sink0):
    nk = 2 * ATT_BLOCK
    zpad = jnp.zeros((nk - wb - seq, nk_ref.shape[1]), F32)
    t = lax.broadcasted_iota(jnp.int32, (seq, nk), 0)
    s = lax.broadcasted_iota(jnp.int32, (seq, nk), 1)
    rel = t + wb - s
    ok = (rel >= 0) & (rel <= WINDOW)
    for e in range(ck_ref.shape[0]):
        rows = slice(e * seq, (e + 1) * seq)
        kcat = jnp.concatenate([ck_ref[e], nk_ref[rows, :], zpad], axis=0)
        vcat = jnp.concatenate([cv_ref[e], nv_ref[rows, :], zpad], axis=0)

        def store(col0, val, rows=rows):
            o_ref[rows, col0:col0 + LANES] = val

        _attend(lambda c0, rows=rows: q_ref[rows, c0:c0 + LANES], kcat, vcat, ok, sink_ref, sink0, seq, store)


def swa_step(q, cache_k, cache_v, kv_new, sinks, j, *, seq, per_step=4):
    rows, DQ = q.shape
    nb, wb, DKV = cache_k.shape
    E = per_step
    return pl.pallas_call(
        functools.partial(_swa_step_kernel, seq=seq, wb=wb, sink0=j * N_Q_HEADS),
        grid=(nb // E,),
        in_specs=[pl.BlockSpec(memory_space=pltpu.SMEM),
                  pl.BlockSpec((E * seq, DQ), lambda b: (b, 0)),
                  pl.BlockSpec((E, wb, DKV), lambda b: (b, 0, 0)), pl.BlockSpec((E * seq, DKV), lambda b: (b, 0)),
                  pl.BlockSpec((E, wb, DKV), lambda b: (b, 0, 0)), pl.BlockSpec((E * seq, DKV), lambda b: (b, 1))],
        out_specs=pl.BlockSpec((E * seq, DQ), lambda b: (b, 0)),
        out_shape=jax.ShapeDtypeStruct((rows, DQ), F32),
        compiler_params=_params(("arbitrary",)),
        name="swa_step",
    )(sinks, q, cache_k, kv_new, cache_v, kv_new)


def _rope_tables_partial(pos):
    half = ROPE_DIM // 2
    inv = ROPE_THETA ** (-jnp.arange(half, dtype=F32) * 2.0 / ROPE_DIM)
    ang = pos.astype(F32)[:, None] * inv[None, :]
    cos, sin = jnp.cos(ang), jnp.sin(ang)
    T = pos.shape[0]
    ones = jnp.ones((T, HEAD_DIM - ROPE_DIM), F32)
    zeros = jnp.zeros((T, HEAD_DIM - ROPE_DIM), F32)
    zh = jnp.zeros((T, half), F32)
    c = jnp.concatenate([cos, cos, ones], axis=1)
    s1 = jnp.concatenate([-sin, zh, zeros], axis=1)
    s2 = jnp.concatenate([zh, sin, zeros], axis=1)
    rep = LANES // HEAD_DIM
    return tuple(jnp.tile(a, (1, rep)) for a in (c, s1, s2))


def _final_norm_kernel(x_ref, g_ref, o_ref):
    x = x_ref[0]
    ms = jnp.mean(x * x, axis=-1, keepdims=True)
    o_ref[0] = x * lax.rsqrt(ms + EPS) * g_ref[...]


def final_norm(x, g, tm):
    B, T, D = x.shape
    return pl.pallas_call(
        _final_norm_kernel,
        grid=(B, T // tm),
        in_specs=[pl.BlockSpec((1, tm, D), lambda b, i: (b, i, 0)), pl.BlockSpec((1, D), lambda b, i: (0, 0))],
        out_specs=pl.BlockSpec((1, tm, D), lambda b, i: (b, i, 0)),
        out_shape=jax.ShapeDtypeStruct((B, T, D), F32),
        compiler_params=_params(("arbitrary", "arbitrary")),
        name="final_norm",
    )(x, g.reshape(1, D))


def kernel(x_prompt, x_sample, state_ret, cache_win_k, cache_win_v, state_conv, c_prompt, c_sample,
           w_ada, b_ada, norm_mix, norm_ffn, ret_w_in, ret_gn, ret_w_out,
           kv_norm, kv_w_ada, kv_b_ada, w_kv, att_w_q, att_sinks, att_w_o,
           ffn_w_up, ffn_conv_w, ffn_conv_b, ffn_w_down, norm_f):
    D = D_MODEL
    BP, TP, _ = x_prompt.shape
    BS, TS, _ = x_sample.shape
    RS = BS * TS
    KV = N_KV_HEADS * HEAD_DIM
    QK = RET_HEADS * RET_DK

    c_all = jnp.concatenate([c_sample, c_prompt], axis=0)
    c_all = jnp.pad(c_all, ((0, -c_all.shape[0] % PACK), (0, 0)))
    mods = Mods(ada_mods(c_all, w_ada, b_ada), BS)
    kv_mods = Mods(ada_mods(c_all, kv_w_ada[None], kv_b_ada[None]), BS)
    tile_col = lambda b, i, j: j

    pos_p = jnp.arange(TP, dtype=jnp.int32)
    pos_s = PAST_LEN + jnp.arange(TS, dtype=jnp.int32)
    tabs_p = _rope_tables_partial(pos_p)
    tabs_s = tuple(jnp.tile(a, (BS, 1)) for a in _rope_tables_partial(pos_s))
    full_p = _rope_tables_full(pos_p)
    full_s = tuple(jnp.tile(a, (BS, 1)) for a in _rope_tables_full(pos_s))
    sinks = att_sinks.reshape(-1)
    kv_norm1, w_kv1 = kv_norm[None], w_kv[None]

    x = x_sample.reshape(1, RS, D)
    wb = cache_win_k.shape[1]
    ck = cache_win_k.reshape(BS, wb, KV)
    cv = cache_win_v.reshape(BS, wb, KV)
    ret_s = None
    conv_s = []
    w16 = [dict() for _ in range(DEPTH)]
    for l in range(DEPTH):
        if l == N_A:
            kv_s, w16_kv = proj(x, kv_norm1, 0, kv_mods.sample(0, 0, D), kv_mods.sample(0, 1, D), w_kv1, 0, tm=RS,
                                tn=2 * KV, tok=TS, out_dtype=F32, rope="partial", rope_tabs=tabs_s, rope_cols=KV,
                                emit_w=True)
            kv_s = kv_s[0]
        sh1, sc1, sh2, sc2 = (mods.sample(l, k, D) for k in (0, 1, 3, 4))
        if l < N_A:
            qkvg, w16[l]["in"] = proj(x, norm_mix, l, sh1, sc1, ret_w_in, l, tm=RS, tn=1024, tok=TS, out_dtype=F32,
                                      rope="full", rope_tabs=full_s, rope_cols=2 * QK, scale_from=QK, emit_w=True)
            a, ret_s = retention_step(qkvg[0], state_ret, ret_gn, l, ret_s, seq=TS)
            x, w16[l]["out"] = out_proj(a[None], ret_w_out, l, x, mods.sample(l, 2, 512, tile_col), tm=RS, tn=512,
                                        tok=TS, emit_w=True)
        else:
            j = l - N_A
            q, w16[l]["in"] = proj(x, norm_mix, l, sh1, sc1, att_w_q, j, tm=RS, tn=1024, tok=TS, out_dtype=F32,
                                   rope="partial", rope_tabs=tabs_s, rope_cols=D, emit_w=True)
            a = swa_step(q[0], ck, cv, kv_s, sinks, j, seq=TS)
            x, w16[l]["out"] = out_proj(a[None], att_w_o, j, x, mods.sample(l, 2, 1024, tile_col), tm=RS, tn=1024,
                                        tok=TS, emit_w=True)
        z, ua, ub, w16[l]["up_a"], w16[l]["up_b"] = upconv_step(x, norm_ffn, l, sh2, sc2, ffn_w_up, ffn_conv_w,
                                                                ffn_conv_b, state_conv, seq=TS)
        conv_s.append(jnp.concatenate([ua.reshape(BS, TS, D_FF)[:, TS - (CONV_W - 1):],
                                       ub.reshape(BS, TS, D_FF)[:, TS - (CONV_W - 1):]], axis=-1))
        x, w16[l]["down"] = out_proj(z[None], ffn_w_down, l, x, mods.sample(l, 5, 512, tile_col), tm=RS, tn=512,
                                     tok=TS, emit_w=True)
    y_sample = final_norm(x, norm_f, RS).reshape(BS, TS, D)

    x = x_prompt
    ret_p = None
    conv_p = []
    for l in range(DEPTH):
        if l == N_A:
            kv_p = proj_prompt(x, kv_norm1, 0, kv_mods, (0, 0, 1), w16_kv, tm=1024, tn=2 * KV, out_dtype=F32,
                               rope="partial", rope_tabs=tabs_p, rope_cols=KV)
        if l < N_A:
            qkvg = proj_prompt(x, norm_mix, l, mods, (l, 0, 1), w16[l]["in"], tm=1024, tn=1024, out_dtype=F32,
                               rope="full", rope_tabs=full_p, rope_cols=2 * QK, scale_from=QK)
            a, ret_p = retention_prompt(qkvg, ret_gn, l, ret_p)
            x = out_proj(a, w16[l]["out"], 0, x, mods.prompt(l, 2, 512, tile_col), tm=1024, tn=512, tok=None)
        else:
            j = l - N_A
            q = proj_prompt(x, norm_mix, l, mods, (l, 0, 1), w16[l]["in"], tm=1024, tn=1024, out_dtype=BF16,
                            rope="partial", rope_tabs=tabs_p, rope_cols=D)
            a = swa_prompt(q, kv_p, sinks, j)
            x = out_proj(a, w16[l]["out"], 0, x, mods.prompt(l, 2, 1024, tile_col), tm=1024, tn=1024, tok=None)
        z, st = upconv_prompt(x, norm_ffn, l, mods, (l, 3, 4), w16[l]["up_a"], w16[l]["up_b"], ffn_conv_w, ffn_conv_b)
        conv_p.append(st[:, -1, :, SUBLANES - (CONV_W - 1):, :].transpose(0, 2, 1, 3).reshape(BP, CONV_W - 1, 2 * D_FF))
        x = out_proj(z, w16[l]["down"], 0, x, mods.prompt(l, 5, 512, tile_col), tm=1024, tn=512, tok=None)
    y_prompt = final_norm(x, norm_f, 512)
    wp = min(WINDOW, TP)
    win_k_prompt = kv_p[:, -wp:, :KV].reshape(BP, wp, N_KV_HEADS, HEAD_DIM)
    win_v_prompt = kv_p[:, -wp:, KV:].reshape(BP, wp, N_KV_HEADS, HEAD_DIM)
    k_all = jnp.concatenate([ck, kv_s[:, :KV].reshape(BS, TS, KV)], axis=1)
    v_all = jnp.concatenate([cv, kv_s[:, KV:].reshape(BS, TS, KV)], axis=1)
    win_k_sample = k_all[:, -wb:].reshape(BS, wb, N_KV_HEADS, HEAD_DIM)
    win_v_sample = v_all[:, -wb:].reshape(BS, wb, N_KV_HEADS, HEAD_DIM)

    return (y_prompt, y_sample, ret_p, ret_s, win_k_prompt, win_v_prompt,
            win_k_sample, win_v_sample, jnp.stack(conv_p), jnp.stack(conv_s))
```

```python
import functools

import jax
import jax.numpy as jnp
from jax import lax
from jax.experimental import pallas as pl
from jax.experimental.pallas import tpu as pltpu

F32 = jnp.float32
BF16 = jnp.bfloat16

D_MODEL = 2048
DEPTH = 4
PAST_LEN = 16384
N_A = DEPTH // 2
RET_HEADS = 8
RET_DK = D_MODEL // RET_HEADS
RET_DV = 2 * D_MODEL // RET_HEADS
RET_ROPE_BASE = 10000.0
N_Q_HEADS = 32
N_KV_HEADS = 4
HEAD_DIM = D_MODEL // N_Q_HEADS
GQA_GROUPS = N_Q_HEADS // N_KV_HEADS
ROPE_DIM = HEAD_DIM // 4
ROPE_THETA = 500000.0
WINDOW = 128
ATT_BLOCK = 128
D_FF = 2 * D_MODEL
CONV_W = 3
N_MOD = 6
EPS = 1e-6
NEG_INF = -1e30

LANES = 128
SUBLANES = 8
PACK = 16
HALO = PACK
RET_BLOCK = 256
RET_HEADS_PER_STEP = 4
RET_PAD = PACK
VMEM_LIMIT = 56 * 1024 * 1024


def _params(sem):
    return pltpu.CompilerParams(dimension_semantics=sem, vmem_limit_bytes=VMEM_LIMIT)


def _silu(x):
    return x * jax.nn.sigmoid(x)


def _bdot(a, b):
    return jnp.dot(a.astype(BF16), b.astype(BF16), preferred_element_type=F32)


def _bdot_nt(a, b):
    return lax.dot_general(a.astype(BF16), b.astype(BF16), (((1,), (1,)), ((), ())), preferred_element_type=F32)


def _norm_mod(x, g, sc, sh):
    ms = jnp.mean(x * x, axis=-1, keepdims=True)
    y = x * lax.rsqrt(ms + EPS) * g
    return y * (1.0 + sc) + sh


class Mods:
    def __init__(self, arr, n_sample):
        self.a3 = arr
        self.a4 = arr.reshape(arr.shape[0], arr.shape[1], 1, arr.shape[2])
        self.n_sample = n_sample

    def prompt(self, l, k, width, col=lambda *g: 0, batch=lambda *g: g[0]):
        per = D_MODEL // width
        off = self.n_sample
        return self.a4, pl.BlockSpec((1, 1, 1, width), lambda *g: (l, off + batch(*g), 0, k * per + col(*g)))

    def sample(self, l, k, width, col=lambda *g: 0):
        per = D_MODEL // width
        return self.a3, pl.BlockSpec((1, self.n_sample, width), lambda *g: (l, 0, k * per + col(*g)))


def _ada_kernel(c_ref, w_ref, b_ref, o_ref):
    o_ref[0] = _bdot(_silu(c_ref[...]), w_ref[0]) + b_ref[0]


def ada_mods(c_all, w, b, tn=1024):
    L, D, N = w.shape
    R = c_all.shape[0]
    return pl.pallas_call(
        _ada_kernel,
        grid=(L, N // tn),
        in_specs=[pl.BlockSpec((R, D), lambda l, j: (0, 0)),
                  pl.BlockSpec((1, D, tn), lambda l, j: (l, 0, j)),
                  pl.BlockSpec((1, 1, tn), lambda l, j: (l, 0, j))],
        out_specs=pl.BlockSpec((1, R, tn), lambda l, j: (l, 0, j)),
        out_shape=jax.ShapeDtypeStruct((L, R, N), F32),
        compiler_params=_params(("arbitrary", "arbitrary")),
        name="ada_mods",
    )(c_all, w, b.reshape(L, 1, N))


def _store_normed(h_scr, r_scr, row0, x_ref, g_ref, sc_ref, sh_ref, rows, tok):
    def stats(r, carry):
        r0 = pl.multiple_of(r * PACK, PACK)
        x = x_ref[0, pl.ds(r0, PACK), :]
        r_scr[pl.ds(r0, PACK), :] = lax.rsqrt(jnp.mean(x * x, axis=-1, keepdims=True) + EPS)
        return carry

    lax.fori_loop(0, rows // PACK, stats, 0, unroll=4)
    g = g_ref[0]

    def piece(start, n, sc, sh):
        return x_ref[0, pl.ds(start, n), :] * r_scr[pl.ds(start, n), :] * g * (1.0 + sc) + sh

    def apply(r, carry):
        r0 = pl.multiple_of(r * PACK, PACK)
        if tok is None:
            h = piece(r0, PACK, sc_ref[0, 0], sh_ref[0, 0])
        else:
            parts = []
            for s in range(PACK // tok):
                m = r * (PACK // tok) + s
                start = pl.multiple_of(r0 + s * tok, tok)
                parts.append(piece(start, tok, sc_ref[0, pl.ds(m, 1), :], sh_ref[0, pl.ds(m, 1), :]))
            h = jnp.concatenate(parts, axis=0)
        h_scr[pl.ds(pl.multiple_of(row0 + r0, PACK), PACK), :] = h.astype(BF16)
        return carry

    lax.fori_loop(0, rows // PACK, apply, 0, unroll=2)


def _rope64(a, c, s1, s2):
    return a * c + pltpu.roll(a, LANES - ROPE_DIM // 2, axis=1) * s1 + pltpu.roll(a, ROPE_DIM // 2, axis=1) * s2


def _proj_kernel(*refs, tm, tn, tok, rope, rope_cols, scale_from, emit_w):
    x_ref, g_ref, sh_ref, sc_ref, w_ref = refs[:5]
    if rope:
        t1_ref, t2_ref, t3_ref = refs[5:8]
    o_ref = refs[-4] if emit_w else refs[-3]
    h_scr, r_scr = refs[-2:]
    j = pl.program_id(2)

    @pl.when(j == 0)
    def _():
        _store_normed(h_scr, r_scr, 0, x_ref, g_ref, sc_ref, sh_ref, tm, tok)

    wb = w_ref[0].astype(BF16)
    if emit_w:
        refs[-3][0] = wb
    acc = jnp.dot(h_scr[...], wb, preferred_element_type=F32)
    _proj_epilogue(acc, o_ref, j, tn, rope, rope_cols, scale_from, refs[5:8])


def _proj_epilogue(acc, o_ref, j, tn, rope, rope_cols, scale_from, tabs):
    if rope is None:
        o_ref[0] = acc.astype(o_ref.dtype)
    elif rope == "partial":
        t1_ref, t2_ref, t3_ref = tabs
        for c in range(tn // LANES):
            a = acc[:, c * LANES:(c + 1) * LANES]
            if c * LANES < rope_cols:
                a = _rope64(a, t1_ref[...], t2_ref[...], t3_ref[...])
            o_ref[0, :, c * LANES:(c + 1) * LANES] = a.astype(o_ref.dtype)
    else:
        t1_ref, t2_ref, _ = tabs
        rotated = j * tn < rope_cols
        cos = jnp.where(rotated, t1_ref[...], 1.0)
        sin = jnp.where(rotated, t2_ref[...], 0.0)
        scale = jnp.where(rotated & (j * tn >= scale_from), RET_DK ** -0.5, 1.0).astype(F32)
        for hd in range(tn // RET_DK):
            x1 = acc[:, hd * RET_DK:hd * RET_DK + LANES]
            x2 = acc[:, hd * RET_DK + LANES:(hd + 1) * RET_DK]
            o_ref[0, :, hd * RET_DK:hd * RET_DK + LANES] = ((x1 * cos - x2 * sin) * scale).astype(o_ref.dtype)
            o_ref[0, :, hd * RET_DK + LANES:(hd + 1) * RET_DK] = ((x2 * cos + x1 * sin) * scale).astype(o_ref.dtype)


def _norm_rows(h_scr, slot, row0, x_ref, g_ref, sc_ref, sh_ref, start, n):
    g = g_ref[0]
    sc1 = 1.0 + sc_ref[0, 0]
    sh = sh_ref[0, 0]
    for c in range(n // PACK):
        r0 = pl.multiple_of(start + c * PACK, PACK)
        x = x_ref[0, pl.ds(r0, PACK), :]
        r = lax.rsqrt(jnp.mean(x * x, axis=-1, keepdims=True) + EPS)
        h_scr[slot, pl.ds(pl.multiple_of(row0 + r0, PACK), PACK), :] = (x * r * g * sc1 + sh).astype(BF16)


def _lag_rows(tm, nj):
    return PACK * -(-tm // (PACK * nj))


def _proj_lag_kernel(*refs, tm, tn, nj, rope, rope_cols, scale_from):
    x_ref, g_ref, sh_ref, sc_ref, w_ref = refs[:5]
    o_ref, h_scr = refs[-2:]
    t = pl.program_id(0)
    j = pl.program_id(1)
    n = _lag_rows(tm, nj)
    start = jnp.minimum(j * n, tm - n)
    wslot = t % 2

    @pl.when(t == 0)
    def _():
        _norm_rows(h_scr, wslot, 0, x_ref, g_ref, sc_ref, sh_ref, start, n)

    @pl.when(t > 0)
    def _():
        _norm_rows(h_scr, wslot, 0, x_ref, g_ref, sc_ref, sh_ref, start, n)
        acc = jnp.dot(h_scr[1 - wslot], w_ref[0], preferred_element_type=F32)
        _proj_epilogue(acc, o_ref, j, tn, rope, rope_cols, scale_from, refs[5:8])


def proj_prompt(x, g, l, mods, ks, w, *, tm, tn, out_dtype, rope=None, rope_tabs=None, rope_cols=0, scale_from=0):
    B, T, D = x.shape
    N = w.shape[2]
    assert rope != "partial" or rope_cols == N or tn == N
    tpb = T // tm
    nt = B * tpb
    nj = N // tn
    cur = lambda t: jnp.minimum(t, nt - 1)
    prev = lambda t: jnp.maximum(t - 1, 0)
    colj = lambda t, j: jnp.where(t > 0, j, 0)
    ml, ksh, ksc = ks
    sh = mods.prompt(ml, ksh, D, batch=lambda t, j: cur(t) // tpb)
    sc = mods.prompt(ml, ksc, D, batch=lambda t, j: cur(t) // tpb)
    in_specs = [pl.BlockSpec((1, tm, D), lambda t, j: (cur(t) // tpb, cur(t) % tpb, 0)),
                pl.BlockSpec((1, 1, D), lambda t, j: (l, 0, 0)),
                sh[1], sc[1],
                pl.BlockSpec((1, D, tn), lambda t, j: (0, 0, colj(t, j)))]
    args = [x, g.reshape(g.shape[0], 1, D), sh[0], sc[0], w]
    if rope:
        tabs = list(rope_tabs) + ([rope_tabs[0]] if len(rope_tabs) == 2 else [])
        in_specs += [pl.BlockSpec((tm, LANES), lambda t, j: (prev(t) % tpb, 0))] * 3
        args += tabs
    return pl.pallas_call(
        functools.partial(_proj_lag_kernel, tm=tm, tn=tn, nj=nj, rope=rope, rope_cols=rope_cols, scale_from=scale_from),
        grid=(nt + 1, nj),
        in_specs=in_specs,
        out_specs=pl.BlockSpec((1, tm, tn), lambda t, j: (prev(t) // tpb, prev(t) % tpb, colj(t, j))),
        out_shape=jax.ShapeDtypeStruct((B, T, N), out_dtype),
        scratch_shapes=[pltpu.VMEM((2, tm, D), BF16)],
        compiler_params=_params(("arbitrary", "arbitrary")),
        name="proj_prompt",
    )(*args)


def proj(x, g, l, sh, sc, w, wl, *, tm, tn, tok, out_dtype, rope=None, rope_tabs=None, rope_cols=0, scale_from=0,
         emit_w=False):
    B, T, D = x.shape
    N = w.shape[2]
    assert rope != "partial" or rope_cols == N or tn == N
    assert not emit_w or (B == 1 and T == tm)
    out_specs = [pl.BlockSpec((1, tm, tn), lambda b, i, j: (b, i, j))]
    out_shape = [jax.ShapeDtypeStruct((B, T, N), out_dtype)]
    if emit_w:
        out_specs.append(pl.BlockSpec((1, D, tn), lambda b, i, j: (0, 0, j)))
        out_shape.append(jax.ShapeDtypeStruct((1, D, N), BF16))
    in_specs = [pl.BlockSpec((1, tm, D), lambda b, i, j: (b, i, 0)),
                pl.BlockSpec((1, 1, D), lambda b, i, j: (l, 0, 0)),
                sh[1], sc[1],
                pl.BlockSpec((1, D, tn), lambda b, i, j: (wl, 0, j))]
    args = [x, g.reshape(g.shape[0], 1, D), sh[0], sc[0], w]
    if rope:
        tabs = list(rope_tabs) + ([rope_tabs[0]] if len(rope_tabs) == 2 else [])
        in_specs += [pl.BlockSpec((tm, LANES), lambda b, i, j: (i, 0))] * 3
        args += tabs
    outs = pl.pallas_call(
        functools.partial(_proj_kernel, tm=tm, tn=tn, tok=tok, rope=rope, rope_cols=rope_cols, scale_from=scale_from,
                          emit_w=emit_w),
        grid=(B, T // tm, N // tn),
        in_specs=in_specs,
        out_specs=out_specs,
        out_shape=out_shape,
        scratch_shapes=[pltpu.VMEM((tm, D), BF16), pltpu.VMEM((tm, 1), F32)],
        compiler_params=_params(("arbitrary", "arbitrary", "arbitrary")),
        name="proj",
    )(*args)
    return outs if emit_w else outs[0]


def _conv_gate(ua, ub, cwa_ref, cwb_ref, cba_ref, cbb_ref):
    def conv(u, cw_ref, cb_ref):
        acc = cb_ref[0] + cw_ref[0, 0:1, :] * u[0]
        acc = acc + cw_ref[0, 1:2, :] * u[1]
        return acc + cw_ref[0, 2:3, :] * u[2]

    return _silu(conv(ua, cwa_ref, cba_ref)) * conv(ub, cwb_ref, cbb_ref)


def _upconv_kernel(x_ref, g_ref, sh_ref, sc_ref, wa_ref, wb_ref, cwa_ref, cwb_ref, cba_ref, cbb_ref,
                   z_ref, st_ref, h_scr, *, tm, nj, tpb, nt):
    t = pl.program_id(0)
    n = _lag_rows(tm, nj)
    start = jnp.minimum(pl.program_id(1) * n, tm - n)
    wslot = t % 2
    zero_halo = jnp.zeros((HALO, h_scr.shape[2]), BF16)

    @pl.when(t == 0)
    def _():
        _norm_rows(h_scr, wslot, HALO, x_ref, g_ref, sc_ref, sh_ref, start, n)
        h_scr[wslot, 0:HALO, :] = zero_halo

    @pl.when(t > 0)
    def _():
        _norm_rows(h_scr, wslot, HALO, x_ref, g_ref, sc_ref, sh_ref, start, n)
        tail = h_scr[1 - wslot, tm:tm + HALO, :]
        h_scr[wslot, 0:HALO, :] = jnp.where(jnp.minimum(t, nt - 1) % tpb > 0, tail, zero_halo)
        h = h_scr[1 - wslot]

        def taps(w_ref):
            u = jnp.dot(h, w_ref[0], preferred_element_type=F32)
            return (pltpu.roll(u, 2, axis=0)[HALO:], pltpu.roll(u, 1, axis=0)[HALO:], u[HALO:])

        ua = taps(wa_ref)
        ub = taps(wb_ref)
        z_ref[0] = _conv_gate(ua, ub, cwa_ref, cwb_ref, cba_ref, cbb_ref).astype(z_ref.dtype)
        st_ref[0, 0, 0] = ua[2][tm - SUBLANES:, :]
        st_ref[0, 0, 1] = ub[2][tm - SUBLANES:, :]


def _halves_specs(shape, l, nj, col):
    return [pl.BlockSpec(shape, lambda *g: (l, 0, col(*g))), pl.BlockSpec(shape, lambda *g: (l, 0, col(*g) + nj))]


def upconv_prompt(x, g, l, mods, ks, wa, wb, conv_w, conv_b, *, tm=1024, tn=512):
    B, T, D = x.shape
    F = wa.shape[2]
    nj = F // tn
    L = conv_w.shape[0]
    tpb = T // tm
    nt = B * tpb
    cur = lambda t: jnp.minimum(t, nt - 1)
    prev = lambda t: jnp.maximum(t - 1, 0)
    colj = lambda t, j: jnp.where(t > 0, j, 0)
    ml, ksh, ksc = ks
    sh = mods.prompt(ml, ksh, D, batch=lambda t, j: cur(t) // tpb)
    sc = mods.prompt(ml, ksc, D, batch=lambda t, j: cur(t) // tpb)
    in_specs = [pl.BlockSpec((1, tm, D), lambda t, j: (cur(t) // tpb, cur(t) % tpb, 0)),
                pl.BlockSpec((1, 1, D), lambda t, j: (l, 0, 0)),
                sh[1], sc[1],
                pl.BlockSpec((1, D, tn), lambda t, j: (0, 0, colj(t, j))),
                pl.BlockSpec((1, D, tn), lambda t, j: (0, 0, colj(t, j)))]
    in_specs += _halves_specs((1, CONV_W, tn), l, nj, colj) + _halves_specs((1, 1, tn), l, nj, colj)
    cb = conv_b.reshape(L, 1, 2 * F)
    return pl.pallas_call(
        functools.partial(_upconv_kernel, tm=tm, nj=nj, tpb=tpb, nt=nt),
        grid=(nt + 1, nj),
        in_specs=in_specs,
        out_specs=[pl.BlockSpec((1, tm, tn), lambda t, j: (prev(t) // tpb, prev(t) % tpb, colj(t, j))),
                   pl.BlockSpec((1, 1, 2, SUBLANES, tn),
                                lambda t, j: (prev(t) // tpb, prev(t) % tpb, 0, 0, colj(t, j)))],
        out_shape=[jax.ShapeDtypeStruct((B, T, F), BF16),
                   jax.ShapeDtypeStruct((B, tpb, 2, SUBLANES, F), F32)],
        scratch_shapes=[pltpu.VMEM((2, HALO + tm, D), BF16)],
        compiler_params=_params(("arbitrary", "arbitrary")),
        name="upconv_prompt",
    )(x, g.reshape(L, 1, D), sh[0], sc[0], wa, wb, conv_w, conv_w, cb, cb)


def _upconv_step_kernel(x_ref, g_ref, sh_ref, sc_ref, wa_ref, wb_ref, cwa_ref, cwb_ref, cba_ref, cbb_ref,
                        sta_ref, stb_ref, z_ref, ua_ref, ub_ref, wao_ref, wbo_ref, h_scr, r_scr, *, rows, seq):
    @pl.when(pl.program_id(0) == 0)
    def _():
        _store_normed(h_scr, r_scr, 0, x_ref, g_ref, sc_ref, sh_ref, rows, seq)

    h = h_scr[...]
    nb = rows // seq

    def taps(w_ref, st_ref, u_ref, wo_ref):
        w = w_ref[0].astype(BF16)
        wo_ref[0] = w
        u = jnp.dot(h, w, preferred_element_type=F32)
        u_ref[...] = u
        u3 = u.reshape(nb, seq, u.shape[1])
        t = lax.broadcasted_iota(jnp.int32, u3.shape, 1)
        st0 = st_ref[0, :, 0:1, :]
        st1 = st_ref[0, :, 1:2, :]
        um1 = jnp.where(t == 0, st1, pltpu.roll(u3, 1, axis=1))
        um2 = jnp.where(t == 0, st0, jnp.where(t == 1, st1, pltpu.roll(u3, 2, axis=1)))
        return (um2, um1, u3)

    ua = taps(wa_ref, sta_ref, ua_ref, wao_ref)
    ub = taps(wb_ref, stb_ref, ub_ref, wbo_ref)
    z = _conv_gate(ua, ub, cwa_ref, cwb_ref, cba_ref, cbb_ref)
    z_ref[...] = z.reshape(rows, z.shape[2]).astype(z_ref.dtype)


def upconv_step(x, g, l, sh, sc, w_up, conv_w, conv_b, state, *, seq, tn=512):
    _, rows, D = x.shape
    F = w_up.shape[2] // 2
    nj = F // tn
    nb = rows // seq
    L = w_up.shape[0]
    col = lambda j: j
    in_specs = [pl.BlockSpec((1, rows, D), lambda j: (0, 0, 0)),
                pl.BlockSpec((1, 1, D), lambda j: (l, 0, 0)),
                sh[1], sc[1]]
    in_specs += _halves_specs((1, D, tn), l, nj, col) + _halves_specs((1, CONV_W, tn), l, nj, col)
    in_specs += _halves_specs((1, 1, tn), l, nj, col)
    in_specs += [pl.BlockSpec((1, nb, CONV_W - 1, tn), lambda j: (l, 0, 0, j)),
                 pl.BlockSpec((1, nb, CONV_W - 1, tn), lambda j: (l, 0, 0, j + nj))]
    cb = conv_b.reshape(L, 1, 2 * F)
    return pl.pallas_call(
        functools.partial(_upconv_step_kernel, rows=rows, seq=seq),
        grid=(nj,),
        in_specs=in_specs,
        out_specs=[pl.BlockSpec((rows, tn), lambda j: (0, j))] * 3 + [pl.BlockSpec((1, D, tn), lambda j: (0, 0, j))] * 2,
        out_shape=[jax.ShapeDtypeStruct((rows, F), BF16), jax.ShapeDtypeStruct((rows, F), F32),
                   jax.ShapeDtypeStruct((rows, F), F32), jax.ShapeDtypeStruct((1, D, F), BF16),
                   jax.ShapeDtypeStruct((1, D, F), BF16)],
        scratch_shapes=[pltpu.VMEM((rows, D), BF16), pltpu.VMEM((rows, 1), F32)],
        compiler_params=_params(("arbitrary",)),
        name="upconv_step",
    )(x, g.reshape(L, 1, D), sh[0], sc[0], w_up, w_up, conv_w, conv_w, cb, cb, state, state)


def _out_kernel(a_ref, w_ref, x_ref, gt_ref, o_ref, *wo_ref, tok):
    w = w_ref[0].astype(BF16)
    if wo_ref:
        wo_ref[0][0] = w
    y = jnp.dot(a_ref[0].astype(BF16), w, preferred_element_type=F32)
    if tok is None:
        o_ref[0] = x_ref[0] + gt_ref[0, 0] * y
    else:
        for b in range(y.shape[0] // tok):
            rows = slice(b * tok, (b + 1) * tok)
            o_ref[0, rows, :] = x_ref[0, rows, :] + gt_ref[0, b:b + 1, :] * y[rows]


def out_proj(a, w, l, x, gate, *, tm, tn, tok, emit_w=False):
    B, T, K = a.shape
    N = w.shape[2]
    assert not emit_w or (B == 1 and T == tm)
    out_specs = [pl.BlockSpec((1, tm, tn), lambda b, i, j: (b, i, j))]
    out_shape = [jax.ShapeDtypeStruct((B, T, N), F32)]
    if emit_w:
        out_specs.append(pl.BlockSpec((1, K, tn), lambda b, i, j: (0, 0, j)))
        out_shape.append(jax.ShapeDtypeStruct((1, K, N), BF16))
    outs = pl.pallas_call(
        functools.partial(_out_kernel, tok=tok),
        grid=(B, T // tm, N // tn),
        in_specs=[pl.BlockSpec((1, tm, K), lambda b, i, j: (b, i, 0)),
                  pl.BlockSpec((1, K, tn), lambda b, i, j: (l, 0, j)),
                  pl.BlockSpec((1, tm, tn), lambda b, i, j: (b, i, j)),
                  gate[1]],
        out_specs=out_specs,
        out_shape=out_shape,
        compiler_params=_params(("arbitrary", "arbitrary", "arbitrary")),
        name="out_proj",
    )(a, w, x, gate[0])
    return outs if emit_w else outs[0]


def _out_final_kernel(a_ref, w_ref, x_ref, gt_ref, nf_ref, o_ref):
    y = jnp.dot(a_ref[0], w_ref[0], preferred_element_type=F32)
    x = x_ref[0] + gt_ref[0, 0] * y
    ms = jnp.mean(x * x, axis=-1, keepdims=True)
    o_ref[0] = x * lax.rsqrt(ms + EPS) * nf_ref[...]


def out_proj_final(a, w, x, gate, norm_g, *, tm):
    B, T, K = a.shape
    N = w.shape[2]
    return pl.pallas_call(
        _out_final_kernel,
        grid=(B, T // tm),
        in_specs=[pl.BlockSpec((1, tm, K), lambda b, i: (b, i, 0)),
                  pl.BlockSpec((1, K, N), lambda b, i: (0, 0, 0), pipeline_mode=pl.Buffered(1)),
                  pl.BlockSpec((1, tm, N), lambda b, i: (b, i, 0)),
                  gate[1],
                  pl.BlockSpec((1, N), lambda b, i: (0, 0))],
        out_specs=pl.BlockSpec((1, tm, N), lambda b, i: (b, i, 0)),
        out_shape=jax.ShapeDtypeStruct((B, T, N), F32),
        compiler_params=_params(("arbitrary", "arbitrary")),
        name="out_proj_final",
    )(a, w, x, gate[0], norm_g.reshape(1, N))


def _rope_full(x, cos, sin):
    x1 = x[:, :LANES]
    x2 = x[:, LANES:]
    return jnp.concatenate([x1 * cos - x2 * sin, x2 * cos + x1 * sin], axis=1)


def _retention_head(q, k, v, gate, S, dec, qd, kd, gl, gn):
    qb = q.astype(BF16)
    vb = v.astype(BF16)
    scores = _bdot_nt(qb, k) * dec
    o = _bdot(scores, vb) + _bdot(qb, S) * qd
    kdt = (k * kd).T
    s_new = gl * S + _bdot(kdt, vb)
    mu = jnp.mean(o, axis=-1, keepdims=True)
    d = o - mu
    var = jnp.mean(d * d, axis=-1, keepdims=True)
    on = d * lax.rsqrt(var + EPS) * gn
    return _silu(gate) * on, s_new


def _ret_kernel(q_ref, k_ref, v_ref, g_ref, dec_ref, qd_ref, kd_ref, gl_ref, gn_ref, *rest, lsel):
    o_ref, s_ref = rest[-2:]

    @pl.when(pl.program_id(2) == 0)
    def _():
        s_ref[...] = jnp.zeros_like(s_ref)

    for hd in range(RET_HEADS_PER_STEP):
        kc = slice(hd * RET_DK, (hd + 1) * RET_DK)
        vc = slice(hd * RET_DV, (hd + 1) * RET_DV)
        o, s_new = _retention_head(q_ref[0, :, kc], k_ref[0, :, kc], v_ref[0, :, vc], g_ref[0, :, vc],
                                   s_ref[lsel, 0, hd], dec_ref[hd], qd_ref[hd], kd_ref[hd], gl_ref[hd],
                                   gn_ref[0, :, vc])
        s_ref[lsel, 0, hd] = s_new
        o_ref[0, :, vc] = o.astype(o_ref.dtype)


def _decay_tables(L, Lp):
    log_gamma = jnp.log(1.0 - jnp.exp2(-5.0 - jnp.arange(RET_HEADS, dtype=F32)))
    idx = jnp.arange(L, dtype=F32)
    rel = idx[:, None] - idx[None, :]
    dec = jnp.where(rel >= 0, jnp.exp(jnp.maximum(rel, 0.0)[None] * log_gamma[:, None, None]), 0.0)
    qd = jnp.exp((idx + 1.0)[None, :] * log_gamma[:, None])[..., None]
    kd = jnp.exp((L - 1.0 - idx)[None, :] * log_gamma[:, None])[..., None]
    gl = jnp.exp(L * log_gamma).reshape(RET_HEADS, 1, 1)
    p = Lp - L
    return (jnp.pad(dec, ((0, 0), (0, p), (0, p))), jnp.pad(qd, ((0, 0), (0, p), (0, 0))),
            jnp.pad(kd, ((0, 0), (0, p), (0, 0))), gl)


def _rope_tables_full(pos):
    half = RET_DK // 2
    inv = RET_ROPE_BASE ** (-jnp.arange(half, dtype=F32) * 2.0 / RET_DK)
    ang = pos.astype(F32)[:, None] * inv[None, :]
    return jnp.cos(ang), jnp.sin(ang)


def _chain(prev, in_specs, args, out_index):
    if prev is None:
        return {}
    in_specs.append(pl.BlockSpec(memory_space=pl.ANY))
    args.append(prev)
    return {len(args) - 1: out_index}


def _layer_block(prev, l):
    return (N_A, 0, l) if prev is None else (1, l, 0)


def retention_prompt(qkvg, gn, l, states):
    B, T, _ = qkvg.shape
    L = RET_BLOCK
    H = RET_HEADS
    P = RET_HEADS_PER_STEP
    dec, qd, kd, gl = _decay_tables(L, L)
    qk_blocks = H // P
    v0 = 2 * H * RET_DK // (P * RET_DV)
    in_specs = [pl.BlockSpec((1, L, P * RET_DK), lambda b, h, c: (b, c, h)),
                pl.BlockSpec((1, L, P * RET_DK), lambda b, h, c: (b, c, qk_blocks + h)),
                pl.BlockSpec((1, L, P * RET_DV), lambda b, h, c: (b, c, v0 + h)),
                pl.BlockSpec((1, L, P * RET_DV), lambda b, h, c: (b, c, v0 + qk_blocks + h)),
                pl.BlockSpec((P, L, L), lambda b, h, c: (h, 0, 0)),
                pl.BlockSpec((P, L, 1), lambda b, h, c: (h, 0, 0)),
                pl.BlockSpec((P, L, 1), lambda b, h, c: (h, 0, 0)),
                pl.BlockSpec((P, 1, 1), lambda b, h, c: (h, 0, 0)),
                pl.BlockSpec((1, 1, P * RET_DV), lambda b, h, c: (l, 0, h))]
    args = [qkvg, qkvg, qkvg, qkvg, dec, qd, kd, gl, gn.reshape(gn.shape[0], 1, -1)]
    aliases = _chain(states, in_specs, args, 1)
    nl, lb, lsel = _layer_block(states, l)
    return pl.pallas_call(
        functools.partial(_ret_kernel, lsel=lsel),
        grid=(B, H // P, T // L),
        in_specs=in_specs,
        out_specs=[pl.BlockSpec((1, L, P * RET_DV), lambda b, h, c: (b, c, h)),
                   pl.BlockSpec((nl, 1, P, RET_DK, RET_DV), lambda b, h, c: (lb, b, h, 0, 0))],
        out_shape=[jax.ShapeDtypeStruct((B, T, H * RET_DV), BF16),
                   jax.ShapeDtypeStruct((N_A, B, H, RET_DK, RET_DV), F32)],
        input_output_aliases=aliases,
        compiler_params=_params(("arbitrary", "arbitrary", "arbitrary")),
        name="retention_prompt",
    )(*args)


def _ret_step_kernel(x_ref, s_ref, dec_ref, qd_ref, kd_ref, gl_ref, gn_ref, *rest, seq, lsel):
    o_ref, so_ref = rest[-2:]
    H = RET_HEADS
    zpad = jnp.zeros((RET_PAD - seq, RET_DV), F32)
    for other in range(so_ref.shape[0]):
        if other != lsel:
            so_ref[other] = jnp.zeros(so_ref.shape[1:], F32)

    def padded(col0, width):
        return jnp.concatenate([x_ref[:, col0:col0 + width], zpad[:, :width]], axis=0)

    for h in range(H):
        q = padded(h * RET_DK, RET_DK)
        k = padded(H * RET_DK + h * RET_DK, RET_DK)
        v = padded(2 * H * RET_DK + h * RET_DV, RET_DV)
        gate = padded(2 * H * RET_DK + H * RET_DV + h * RET_DV, RET_DV)
        o, s_new = _retention_head(q, k, v, gate, s_ref[0, 0, h], dec_ref[h], qd_ref[h], kd_ref[h],
                                   gl_ref[h], gn_ref[0, :, h * RET_DV:(h + 1) * RET_DV])
        so_ref[lsel, 0, h] = s_new
        o_ref[:, h * RET_DV:(h + 1) * RET_DV] = o[:seq]


def retention_step(qkvg, state, gn, l, states, *, seq):
    rows, width = qkvg.shape
    nb = rows // seq
    H = RET_HEADS
    dec, qd, kd, gl = _decay_tables(seq, RET_PAD)
    whole = lambda shape: pl.BlockSpec(shape, lambda b: (0,) * len(shape))
    in_specs = [pl.BlockSpec((seq, width), lambda b: (b, 0)),
                pl.BlockSpec((1, 1, H, RET_DK, RET_DV), lambda b: (l, b, 0, 0, 0)),
                whole((H, RET_PAD, RET_PAD)), whole((H, RET_PAD, 1)), whole((H, RET_PAD, 1)), whole((H, 1, 1)),
                pl.BlockSpec((1, 1, H * RET_DV), lambda b: (l, 0, 0))]
    args = [qkvg, state, dec, qd, kd, gl, gn.reshape(gn.shape[0], 1, -1)]
    aliases = _chain(states, in_specs, args, 1)
    nl, lb, lsel = _layer_block(states, l)
    return pl.pallas_call(
        functools.partial(_ret_step_kernel, seq=seq, lsel=lsel),
        grid=(nb,),
        in_specs=in_specs,
        out_specs=[pl.BlockSpec((seq, H * RET_DV), lambda b: (b, 0)),
                   pl.BlockSpec((nl, 1, H, RET_DK, RET_DV), lambda b: (lb, b, 0, 0, 0))],
        out_shape=[jax.ShapeDtypeStruct((rows, H * RET_DV), F32),
                   jax.ShapeDtypeStruct(state.shape, F32)],
        input_output_aliases=aliases,
        compiler_params=_params(("arbitrary",)),
        name="retention_step",
    )(*args)


def _head_pairs(kcat, vcat):
    lo = lax.broadcasted_iota(jnp.int32, (kcat.shape[0], LANES), 1) < HEAD_DIM
    for pair in range(N_KV_HEADS // 2):
        kp = kcat[:, pair * LANES:(pair + 1) * LANES]
        vp = vcat[:, pair * LANES:(pair + 1) * LANES]
        kr = pltpu.roll(kp, HEAD_DIM, axis=1)
        vr = pltpu.roll(vp, HEAD_DIM, axis=1)
        for sub in range(2):
            ka, kb_ = (kp, kr) if sub == 0 else (kr, kp)
            va, vb_ = (vp, vr) if sub == 0 else (vr, vp)
            halves = [(jnp.where(lo, ka, 0.0).astype(BF16), jnp.where(lo, va, 0.0).astype(BF16)),
                      (jnp.where(lo, 0.0, kb_).astype(BF16), jnp.where(lo, 0.0, vb_).astype(BF16))]
            yield pair * 2 + sub, halves


def _sink_softmax(s, ok, sink):
    s = jnp.where(ok, s * (HEAD_DIM ** -0.5), NEG_INF)
    m = jnp.maximum(jnp.max(s, axis=-1, keepdims=True), sink)
    p = jnp.exp(s - m)
    return p * (1.0 / (jnp.sum(p, axis=-1, keepdims=True) + jnp.exp(sink - m)))


def _attend(q_of, kcat, vcat, ok, sink_ref, sink0, rows, store):
    pieces = GQA_GROUPS // 2
    for kh, halves in _head_pairs(kcat, vcat):
        base = kh * GQA_GROUPS * HEAD_DIM
        qs = jnp.concatenate([q_of(base + g * LANES) for g in range(pieces)], axis=0).astype(BF16)
        out = None
        for half, (kk, vv) in enumerate(halves):
            s = _bdot_nt(qs, kk)
            p = jnp.concatenate(
                [_sink_softmax(s[g * rows:(g + 1) * rows], ok, sink_ref[sink0 + kh * GQA_GROUPS + 2 * g + half])
                 for g in range(pieces)], axis=0)
            o = _bdot(p, vv)
            out = o if out is None else out + o
        for g in range(pieces):
            store(base + g * LANES, out[g * rows:(g + 1) * rows])


def _swa_kernel(sink_ref, q_ref, kp_ref, kc_ref, vp_ref, vc_ref, o_ref, *, sink0):
    i = pl.program_id(1)
    kcat = jnp.concatenate([kp_ref[0], kc_ref[0]], axis=0)
    vcat = jnp.concatenate([vp_ref[0], vc_ref[0]], axis=0)
    ql = lax.broadcasted_iota(jnp.int32, (ATT_BLOCK, 2 * ATT_BLOCK), 0)
    km = lax.broadcasted_iota(jnp.int32, (ATT_BLOCK, 2 * ATT_BLOCK), 1)
    no_prev = jnp.where(i > 0, 0, 4 * ATT_BLOCK)
    ok = ((km < ATT_BLOCK) & (km >= ql + no_prev)) | ((km >= ATT_BLOCK) & (km - ATT_BLOCK <= ql))

    def store(col0, val):
        o_ref[0, :, col0:col0 + LANES] = val.astype(o_ref.dtype)

    _attend(lambda c0: q_ref[0, :, c0:c0 + LANES], kcat, vcat, ok, sink_ref, sink0, ATT_BLOCK, store)


def swa_prompt(q, kv, sinks, j):
    B, T, DQ = q.shape
    DKV = kv.shape[2] // 2
    cur = lambda b, i: (b, i, 0)
    return pl.pallas_call(
        functools.partial(_swa_kernel, sink0=j * N_Q_HEADS),
        grid=(B, T // ATT_BLOCK),
        in_specs=[pl.BlockSpec(memory_space=pltpu.SMEM),
                  pl.BlockSpec((1, ATT_BLOCK, DQ), cur),
                  pl.BlockSpec((1, ATT_BLOCK, DKV), lambda b, i: (b, jnp.maximum(i - 1, 0), 0)),
                  pl.BlockSpec((1, ATT_BLOCK, DKV), lambda b, i: (b, i, 0)),
                  pl.BlockSpec((1, ATT_BLOCK, DKV), lambda b, i: (b, jnp.maximum(i - 1, 0), 1)),
                  pl.BlockSpec((1, ATT_BLOCK, DKV), lambda b, i: (b, i, 1))],
        out_specs=pl.BlockSpec((1, ATT_BLOCK, DQ), cur),
        out_shape=jax.ShapeDtypeStruct((B, T, DQ), BF16),
        compiler_params=_params(("arbitrary", "arbitrary")),
        name="swa_prompt",
    )(sinks, q, kv, kv, kv, kv)


def _swa_step_kernel(sink_ref, q_ref, ck_ref, nk_ref, cv_ref, nv_ref, o_ref, *, seq, wb, sink0):
    nk = 2 * ATT_BLOCK
    zpad = jnp.zeros((nk - wb - seq, nk_ref.shape[1]), F32)
    t = lax.broadcasted_iota(jnp.int32, (seq, nk), 0)
    s = lax.broadcasted_iota(jnp.int32, (seq, nk), 1)
    rel = t + wb - s
    ok = (rel >= 0) & (rel <= WINDOW)
    for e in range(ck_ref.shape[0]):
        rows = slice(e * seq, (e + 1) * seq)
        kcat = jnp.concatenate([ck_ref[e], nk_ref[rows, :], zpad], axis=0)
        vcat = jnp.concatenate([cv_ref[e], nv_ref[rows, :], zpad], axis=0)

        def store(col0, val, rows=rows):
            o_ref[rows, col0:col0 + LANES] = val

        _attend(lambda c0, rows=rows: q_ref[rows, c0:c0 + LANES], kcat, vcat, ok, sink_ref, sink0, seq, store)


def swa_step(q, cache_k, cache_v, kv_new, sinks, j, *, seq, per_step=4):
    rows, DQ = q.shape
    nb, wb, DKV = cache_k.shape
    E = per_step
    return pl.pallas_call(
        functools.partial(_swa_step_kernel, seq=seq, wb=wb, sink0=j * N_Q_HEADS),
        grid=(nb // E,),
        in_specs=[pl.BlockSpec(memory_space=pltpu.SMEM),
                  pl.BlockSpec((E * seq, DQ), lambda b: (b, 0)),
                  pl.BlockSpec((E, wb, DKV), lambda b: (b, 0, 0)), pl.BlockSpec((E * seq, DKV), lambda b: (b, 0)),
                  pl.BlockSpec((E, wb, DKV), lambda b: (b, 0, 0)), pl.BlockSpec((E * seq, DKV), lambda b: (b, 1))],
        out_specs=pl.BlockSpec((E * seq, DQ), lambda b: (b, 0)),
        out_shape=jax.ShapeDtypeStruct((rows, DQ), F32),
        compiler_params=_params(("arbitrary",)),
        name="swa_step",
    )(sinks, q, cache_k, kv_new, cache_v, kv_new)


def _rope_tables_partial(pos):
    half = ROPE_DIM // 2
    inv = ROPE_THETA ** (-jnp.arange(half, dtype=F32) * 2.0 / ROPE_DIM)
    ang = pos.astype(F32)[:, None] * inv[None, :]
    cos, sin = jnp.cos(ang), jnp.sin(ang)
    T = pos.shape[0]
    ones = jnp.ones((T, HEAD_DIM - ROPE_DIM), F32)
    zeros = jnp.zeros((T, HEAD_DIM - ROPE_DIM), F32)
    zh = jnp.zeros((T, half), F32)
    c = jnp.concatenate([cos, cos, ones], axis=1)
    s1 = jnp.concatenate([-sin, zh, zeros], axis=1)
    s2 = jnp.concatenate([zh, sin, zeros], axis=1)
    rep = LANES // HEAD_DIM
    return tuple(jnp.tile(a, (1, rep)) for a in (c, s1, s2))


def _final_norm_kernel(x_ref, g_ref, o_ref):
    x = x_ref[0]
    ms = jnp.mean(x * x, axis=-1, keepdims=True)
    o_ref[0] = x * lax.rsqrt(ms + EPS) * g_ref[...]


def final_norm(x, g, tm):
    B, T, D = x.shape
    return pl.pallas_call(
        _final_norm_kernel,
        grid=(B, T // tm),
        in_specs=[pl.BlockSpec((1, tm, D), lambda b, i: (b, i, 0)), pl.BlockSpec((1, D), lambda b, i: (0, 0))],
        out_specs=pl.BlockSpec((1, tm, D), lambda b, i: (b, i, 0)),
        out_shape=jax.ShapeDtypeStruct((B, T, D), F32),
        compiler_params=_params(("arbitrary", "arbitrary")),
        name="final_norm",
    )(x, g.reshape(1, D))


def kernel(x_prompt, x_sample, state_ret, cache_win_k, cache_win_v, state_conv, c_prompt, c_sample,
           w_ada, b_ada, norm_mix, norm_ffn, ret_w_in, ret_gn, ret_w_out,
           kv_norm, kv_w_ada, kv_b_ada, w_kv, att_w_q, att_sinks, att_w_o,
           ffn_w_up, ffn_conv_w, ffn_conv_b, ffn_w_down, norm_f):
    D = D_MODEL
    BP, TP, _ = x_prompt.shape
    BS, TS, _ = x_sample.shape
    RS = BS * TS
    KV = N_KV_HEADS * HEAD_DIM
    QK = RET_HEADS * RET_DK

    c_all = jnp.concatenate([c_sample, c_prompt], axis=0)
    c_all = jnp.pad(c_all, ((0, -c_all.shape[0] % PACK), (0, 0)))
    mods = Mods(ada_mods(c_all, w_ada, b_ada), BS)
    kv_mods = Mods(ada_mods(c_all, kv_w_ada[None], kv_b_ada[None]), BS)
    tile_col = lambda b, i, j: j

    pos_p = jnp.arange(TP, dtype=jnp.int32)
    pos_s = PAST_LEN + jnp.arange(TS, dtype=jnp.int32)
    tabs_p = _rope_tables_partial(pos_p)
    tabs_s = tuple(jnp.tile(a, (BS, 1)) for a in _rope_tables_partial(pos_s))
    full_p = _rope_tables_full(pos_p)
    full_s = tuple(jnp.tile(a, (BS, 1)) for a in _rope_tables_full(pos_s))
    sinks = att_sinks.reshape(-1)
    kv_norm1, w_kv1 = kv_norm[None], w_kv[None]

    x = x_sample.reshape(1, RS, D)
    wb = cache_win_k.shape[1]
    ck = cache_win_k.reshape(BS, wb, KV)
    cv = cache_win_v.reshape(BS, wb, KV)
    ret_s = None
    conv_s = []
    w16 = [dict() for _ in range(DEPTH)]
    for l in range(DEPTH):
        if l == N_A:
            kv_s, w16_kv = proj(x, kv_norm1, 0, kv_mods.sample(0, 0, D), kv_mods.sample(0, 1, D), w_kv1, 0, tm=RS,
                                tn=2 * KV, tok=TS, out_dtype=F32, rope="partial", rope_tabs=tabs_s, rope_cols=KV,
                                emit_w=True)
            kv_s = kv_s[0]
        sh1, sc1, sh2, sc2 = (mods.sample(l, k, D) for k in (0, 1, 3, 4))
        if l < N_A:
            qkvg, w16[l]["in"] = proj(x, norm_mix, l, sh1, sc1, ret_w_in, l, tm=RS, tn=1024, tok=TS, out_dtype=F32,
                                      rope="full", rope_tabs=full_s, rope_cols=2 * QK, scale_from=QK, emit_w=True)
            a, ret_s = retention_step(qkvg[0], state_ret, ret_gn, l, ret_s, seq=TS)
            x, w16[l]["out"] = out_proj(a[None], ret_w_out, l, x, mods.sample(l, 2, 512, tile_col), tm=RS, tn=512,
                                        tok=TS, emit_w=True)
        else:
            j = l - N_A
            q, w16[l]["in"] = proj(x, norm_mix, l, sh1, sc1, att_w_q, j, tm=RS, tn=1024, tok=TS, out_dtype=F32,
                                   rope="partial", rope_tabs=tabs_s, rope_cols=D, emit_w=True)
            a = swa_step(q[0], ck, cv, kv_s, sinks, j, seq=TS)
            x, w16[l]["out"] = out_proj(a[None], att_w_o, j, x, mods.sample(l, 2, 1024, tile_col), tm=RS, tn=1024,
                                        tok=TS, emit_w=True)
        z, ua, ub, w16[l]["up_a"], w16[l]["up_b"] = upconv_step(x, norm_ffn, l, sh2, sc2, ffn_w_up, ffn_conv_w,
                                                                ffn_conv_b, state_conv, seq=TS)
        conv_s.append(jnp.concatenate([ua.reshape(BS, TS, D_FF)[:, TS - (CONV_W - 1):],
                                       ub.reshape(BS, TS, D_FF)[:, TS - (CONV_W - 1):]], axis=-1))
        x, w16[l]["down"] = out_proj(z[None], ffn_w_down, l, x, mods.sample(l, 5, 512, tile_col), tm=RS, tn=512,
                                     tok=TS, emit_w=True)
    y_sample = final_norm(x, norm_f, RS).reshape(BS, TS, D)

    x = x_prompt
    ret_p = None
    conv_p = []
    for l in range(DEPTH):
        if l == N_A:
            kv_p = proj_prompt(x, kv_norm1, 0, kv_mods, (0, 0, 1), w16_kv, tm=1024, tn=2 * KV, out_dtype=F32,
                               rope="partial", rope_tabs=tabs_p, rope_cols=KV)
        if l < N_A:
            qkvg = proj_prompt(x, norm_mix, l, mods, (l, 0, 1), w16[l]["in"], tm=1024, tn=1024, out_dtype=F32,
                               rope="full", rope_tabs=full_p, rope_cols=2 * QK, scale_from=QK)
            a, ret_p = retention_prompt(qkvg, ret_gn, l, ret_p)
            x = out_proj(a, w16[l]["out"], 0, x, mods.prompt(l, 2, 512, tile_col), tm=1024, tn=512, tok=None)
        else:
            j = l - N_A
            q = proj_prompt(x, norm_mix, l, mods, (l, 0, 1), w16[l]["in"], tm=1024, tn=1024, out_dtype=BF16,
                            rope="partial", rope_tabs=tabs_p, rope_cols=D)
            a = swa_prompt(q, kv_p, sinks, j)
            x = out_proj(a, w16[l]["out"], 0, x, mods.prompt(l, 2, 1024, tile_col), tm=1024, tn=1024, tok=None)
        z, st = upconv_prompt(x, norm_ffn, l, mods, (l, 3, 4), w16[l]["up_a"], w16[l]["up_b"], ffn_conv_w, ffn_conv_b)
        conv_p.append(st[:, -1, :, SUBLANES - (CONV_W - 1):, :].transpose(0, 2, 1, 3).reshape(BP, CONV_W - 1, 2 * D_FF))
        if l < DEPTH - 1:
            x = out_proj(z, w16[l]["down"], 0, x, mods.prompt(l, 5, 512, tile_col), tm=1024, tn=512, tok=None)
        else:
            y_prompt = out_proj_final(z, w16[l]["down"], x, mods.prompt(l, 5, D), norm_f, tm=512)
    wp = min(WINDOW, TP)
    win_k_prompt = kv_p[:, -wp:, :KV].reshape(BP, wp, N_KV_HEADS, HEAD_DIM)
    win_v_prompt = kv_p[:, -wp:, KV:].reshape(BP, wp, N_KV_HEADS, HEAD_DIM)
    k_all = jnp.concatenate([ck, kv_s[:, :KV].reshape(BS, TS, KV)], axis=1)
    v_all = jnp.concatenate([cv, kv_s[:, KV:].reshape(BS, TS, KV)], axis=1)
    win_k_sample = k_all[:, -wb:].reshape(BS, wb, N_KV_HEADS, HEAD_DIM)
    win_v_sample = v_all[:, -wb:].reshape(BS, wb, N_KV_HEADS, HEAD_DIM)

    return (y_prompt, y_sample, ret_p, ret_s, win_k_prompt, win_v_prompt,
            win_k_sample, win_v_sample, jnp.stack(conv_p), jnp.stack(conv_s))
```

```python
import functools

import jax
import jax.numpy as jnp
from jax import lax
from jax.experimental import pallas as pl
from jax.experimental.pallas import tpu as pltpu

F32 = jnp.float32
BF16 = jnp.bfloat16

D_MODEL = 2048
DEPTH = 4
PAST_LEN = 16384
N_A = DEPTH // 2
RET_HEADS = 8
RET_DK = D_MODEL // RET_HEADS
RET_DV = 2 * D_MODEL // RET_HEADS
RET_ROPE_BASE = 10000.0
N_Q_HEADS = 32
N_KV_HEADS = 4
HEAD_DIM = D_MODEL // N_Q_HEADS
GQA_GROUPS = N_Q_HEADS // N_KV_HEADS
ROPE_DIM = HEAD_DIM // 4
ROPE_THETA = 500000.0
WINDOW = 128
ATT_BLOCK = 128
D_FF = 2 * D_MODEL
CONV_W = 3
N_MOD = 6
EPS = 1e-6
NEG_INF = -1e30

LANES = 128
SUBLANES = 8
PACK = 16
HALO = PACK
RET_BLOCK = 256
RET_HEADS_PER_STEP = 4
RET_PAD = PACK
VMEM_LIMIT = 56 * 1024 * 1024
PROMPT_TM = 1024
PROMPT_TN_PROJ = 1024
PROMPT_TN_OUT = 1024
PROMPT_TN_UP = 512
PROMPT_TM_FINAL = 512
SAMPLE_TN_PROJ = 1024
SAMPLE_TN_UP = 512
SAMPLE_TN_OUT = {RET_HEADS * RET_DV: 512, D_MODEL: 1024}


def _params(sem):
    return pltpu.CompilerParams(dimension_semantics=sem, vmem_limit_bytes=VMEM_LIMIT)


def _silu(x):
    return x * jax.nn.sigmoid(x)


def _bdot(a, b):
    return jnp.dot(a.astype(BF16), b.astype(BF16), preferred_element_type=F32)


def _bdot_nt(a, b):
    return lax.dot_general(a.astype(BF16), b.astype(BF16), (((1,), (1,)), ((), ())), preferred_element_type=F32)


def _norm_mod(x, g, sc, sh):
    ms = jnp.mean(x * x, axis=-1, keepdims=True)
    y = x * lax.rsqrt(ms + EPS) * g
    return y * (1.0 + sc) + sh


class Mods:
    def __init__(self, arr, n_sample):
        self.a3 = arr
        self.a4 = arr.reshape(arr.shape[0], arr.shape[1], 1, arr.shape[2])
        self.n_sample = n_sample

    def prompt(self, l, k, width, col=lambda *g: 0, batch=lambda *g: g[0]):
        per = D_MODEL // width
        off = self.n_sample
        return self.a4, pl.BlockSpec((1, 1, 1, width), lambda *g: (l, off + batch(*g), 0, k * per + col(*g)))

    def sample(self, l, k, width, col=lambda *g: 0):
        per = D_MODEL // width
        return self.a3, pl.BlockSpec((1, self.n_sample, width), lambda *g: (l, 0, k * per + col(*g)))


def _ada_kernel(c_ref, w_ref, b_ref, o_ref):
    o_ref[0] = _bdot(_silu(c_ref[...]), w_ref[0]) + b_ref[0]


def ada_mods(c_all, w, b, tn=1024):
    L, D, N = w.shape
    R = c_all.shape[0]
    return pl.pallas_call(
        _ada_kernel,
        grid=(L, N // tn),
        in_specs=[pl.BlockSpec((R, D), lambda l, j: (0, 0)),
                  pl.BlockSpec((1, D, tn), lambda l, j: (l, 0, j)),
                  pl.BlockSpec((1, 1, tn), lambda l, j: (l, 0, j))],
        out_specs=pl.BlockSpec((1, R, tn), lambda l, j: (l, 0, j)),
        out_shape=jax.ShapeDtypeStruct((L, R, N), F32),
        compiler_params=_params(("arbitrary", "arbitrary")),
        name="ada_mods",
    )(c_all, w, b.reshape(L, 1, N))


def _store_normed(h_scr, r_scr, row0, x_ref, g_ref, sc_ref, sh_ref, rows, tok):
    def stats(r, carry):
        r0 = pl.multiple_of(r * PACK, PACK)
        x = x_ref[0, pl.ds(r0, PACK), :]
        r_scr[pl.ds(r0, PACK), :] = lax.rsqrt(jnp.mean(x * x, axis=-1, keepdims=True) + EPS)
        return carry

    lax.fori_loop(0, rows // PACK, stats, 0, unroll=4)
    g = g_ref[0]

    def piece(start, n, sc, sh):
        return x_ref[0, pl.ds(start, n), :] * r_scr[pl.ds(start, n), :] * g * (1.0 + sc) + sh

    def apply(r, carry):
        r0 = pl.multiple_of(r * PACK, PACK)
        if tok is None:
            h = piece(r0, PACK, sc_ref[0, 0], sh_ref[0, 0])
        else:
            parts = []
            for s in range(PACK // tok):
                m = r * (PACK // tok) + s
                start = pl.multiple_of(r0 + s * tok, tok)
                parts.append(piece(start, tok, sc_ref[0, pl.ds(m, 1), :], sh_ref[0, pl.ds(m, 1), :]))
            h = jnp.concatenate(parts, axis=0)
        h_scr[pl.ds(pl.multiple_of(row0 + r0, PACK), PACK), :] = h.astype(BF16)
        return carry

    lax.fori_loop(0, rows // PACK, apply, 0, unroll=2)


def _rope64(a, c, s1, s2):
    return a * c + pltpu.roll(a, LANES - ROPE_DIM // 2, axis=1) * s1 + pltpu.roll(a, ROPE_DIM // 2, axis=1) * s2


def _proj_kernel(*refs, tm, tn, tok, rope, rope_cols, scale_from, emit_w):
    x_ref, g_ref, sh_ref, sc_ref, w_ref = refs[:5]
    if rope:
        t1_ref, t2_ref, t3_ref = refs[5:8]
    o_ref = refs[-4] if emit_w else refs[-3]
    h_scr, r_scr = refs[-2:]
    j = pl.program_id(2)

    @pl.when(j == 0)
    def _():
        _store_normed(h_scr, r_scr, 0, x_ref, g_ref, sc_ref, sh_ref, tm, tok)

    wb = w_ref[0].astype(BF16)
    if emit_w:
        refs[-3][0] = wb
    acc = jnp.dot(h_scr[...], wb, preferred_element_type=F32)
    _proj_epilogue(acc, o_ref, j, tn, rope, rope_cols, scale_from, refs[5:8])


def _proj_epilogue(acc, o_ref, j, tn, rope, rope_cols, scale_from, tabs):
    if rope is None:
        o_ref[0] = acc.astype(o_ref.dtype)
    elif rope == "partial":
        t1_ref, t2_ref, t3_ref = tabs
        for c in range(tn // LANES):
            a = acc[:, c * LANES:(c + 1) * LANES]
            if c * LANES < rope_cols:
                a = _rope64(a, t1_ref[...], t2_ref[...], t3_ref[...])
            o_ref[0, :, c * LANES:(c + 1) * LANES] = a.astype(o_ref.dtype)
    else:
        t1_ref, t2_ref, _ = tabs
        rotated = j * tn < rope_cols
        cos = jnp.where(rotated, t1_ref[...], 1.0)
        sin = jnp.where(rotated, t2_ref[...], 0.0)
        scale = jnp.where(rotated & (j * tn >= scale_from), RET_DK ** -0.5, 1.0).astype(F32)
        for hd in range(tn // RET_DK):
            x1 = acc[:, hd * RET_DK:hd * RET_DK + LANES]
            x2 = acc[:, hd * RET_DK + LANES:(hd + 1) * RET_DK]
            o_ref[0, :, hd * RET_DK:hd * RET_DK + LANES] = ((x1 * cos - x2 * sin) * scale).astype(o_ref.dtype)
            o_ref[0, :, hd * RET_DK + LANES:(hd + 1) * RET_DK] = ((x2 * cos + x1 * sin) * scale).astype(o_ref.dtype)


def _norm_rows(h_scr, slot, row0, x_ref, g_ref, sc_ref, sh_ref, start, n):
    g = g_ref[0]
    sc1 = 1.0 + sc_ref[0, 0]
    sh = sh_ref[0, 0]
    for c in range(n // PACK):
        r0 = pl.multiple_of(start + c * PACK, PACK)
        x = x_ref[0, pl.ds(r0, PACK), :]
        r = lax.rsqrt(jnp.mean(x * x, axis=-1, keepdims=True) + EPS)
        h_scr[slot, pl.ds(pl.multiple_of(row0 + r0, PACK), PACK), :] = (x * r * g * sc1 + sh).astype(BF16)


def _lag_rows(tm, nj):
    return PACK * -(-tm // (PACK * nj))


def _proj_lag_kernel(*refs, tm, tn, nj, rope, rope_cols, scale_from):
    x_ref, g_ref, sh_ref, sc_ref, w_ref = refs[:5]
    o_ref, h_scr = refs[-2:]
    t = pl.program_id(0)
    j = pl.program_id(1)
    n = _lag_rows(tm, nj)
    start = jnp.minimum(j * n, tm - n)
    wslot = t % 2

    @pl.when(t == 0)
    def _():
        _norm_rows(h_scr, wslot, 0, x_ref, g_ref, sc_ref, sh_ref, start, n)

    @pl.when(t > 0)
    def _():
        _norm_rows(h_scr, wslot, 0, x_ref, g_ref, sc_ref, sh_ref, start, n)
        acc = jnp.dot(h_scr[1 - wslot], w_ref[0], preferred_element_type=F32)
        _proj_epilogue(acc, o_ref, j, tn, rope, rope_cols, scale_from, refs[5:8])


def proj_prompt(x, g, l, mods, ks, w, *, tm, tn, out_dtype, rope=None, rope_tabs=None, rope_cols=0, scale_from=0):
    B, T, D = x.shape
    N = w.shape[2]
    assert rope != "partial" or rope_cols == N or tn == N
    tpb = T // tm
    nt = B * tpb
    nj = N // tn
    cur = lambda t: jnp.minimum(t, nt - 1)
    prev = lambda t: jnp.maximum(t - 1, 0)
    colj = lambda t, j: jnp.where(t > 0, j, 0)
    ml, ksh, ksc = ks
    sh = mods.prompt(ml, ksh, D, batch=lambda t, j: cur(t) // tpb)
    sc = mods.prompt(ml, ksc, D, batch=lambda t, j: cur(t) // tpb)
    in_specs = [pl.BlockSpec((1, tm, D), lambda t, j: (cur(t) // tpb, cur(t) % tpb, 0)),
                pl.BlockSpec((1, 1, D), lambda t, j: (l, 0, 0)),
                sh[1], sc[1],
                pl.BlockSpec((1, D, tn), lambda t, j: (0, 0, colj(t, j)))]
    args = [x, g.reshape(g.shape[0], 1, D), sh[0], sc[0], w]
    if rope:
        tabs = list(rope_tabs) + ([rope_tabs[0]] if len(rope_tabs) == 2 else [])
        in_specs += [pl.BlockSpec((tm, LANES), lambda t, j: (prev(t) % tpb, 0))] * 3
        args += tabs
    return pl.pallas_call(
        functools.partial(_proj_lag_kernel, tm=tm, tn=tn, nj=nj, rope=rope, rope_cols=rope_cols, scale_from=scale_from),
        grid=(nt + 1, nj),
        in_specs=in_specs,
        out_specs=pl.BlockSpec((1, tm, tn), lambda t, j: (prev(t) // tpb, prev(t) % tpb, colj(t, j))),
        out_shape=jax.ShapeDtypeStruct((B, T, N), out_dtype),
        scratch_shapes=[pltpu.VMEM((2, tm, D), BF16)],
        compiler_params=_params(("arbitrary", "arbitrary")),
        name="proj_prompt",
    )(*args)


def proj(x, g, l, sh, sc, w, wl, *, tm, tn, tok, out_dtype, rope=None, rope_tabs=None, rope_cols=0, scale_from=0,
         emit_w=False):
    B, T, D = x.shape
    N = w.shape[2]
    assert rope != "partial" or rope_cols == N or tn == N
    assert not emit_w or (B == 1 and T == tm)
    out_specs = [pl.BlockSpec((1, tm, tn), lambda b, i, j: (b, i, j))]
    out_shape = [jax.ShapeDtypeStruct((B, T, N), out_dtype)]
    if emit_w:
        out_specs.append(pl.BlockSpec((1, D, tn), lambda b, i, j: (0, 0, j)))
        out_shape.append(jax.ShapeDtypeStruct((1, D, N), BF16))
    in_specs = [pl.BlockSpec((1, tm, D), lambda b, i, j: (b, i, 0)),
                pl.BlockSpec((1, 1, D), lambda b, i, j: (l, 0, 0)),
                sh[1], sc[1],
                pl.BlockSpec((1, D, tn), lambda b, i, j: (wl, 0, j))]
    args = [x, g.reshape(g.shape[0], 1, D), sh[0], sc[0], w]
    if rope:
        tabs = list(rope_tabs) + ([rope_tabs[0]] if len(rope_tabs) == 2 else [])
        in_specs += [pl.BlockSpec((tm, LANES), lambda b, i, j: (i, 0))] * 3
        args += tabs
    outs = pl.pallas_call(
        functools.partial(_proj_kernel, tm=tm, tn=tn, tok=tok, rope=rope, rope_cols=rope_cols, scale_from=scale_from,
                          emit_w=emit_w),
        grid=(B, T // tm, N // tn),
        in_specs=in_specs,
        out_specs=out_specs,
        out_shape=out_shape,
        scratch_shapes=[pltpu.VMEM((tm, D), BF16), pltpu.VMEM((tm, 1), F32)],
        compiler_params=_params(("arbitrary", "arbitrary", "arbitrary")),
        name="proj",
    )(*args)
    return outs if emit_w else outs[0]


def _conv_gate(ua, ub, cwa_ref, cwb_ref, cba_ref, cbb_ref):
    def conv(u, cw_ref, cb_ref):
        acc = cb_ref[0] + cw_ref[0, 0:1, :] * u[0]
        acc = acc + cw_ref[0, 1:2, :] * u[1]
        return acc + cw_ref[0, 2:3, :] * u[2]

    return _silu(conv(ua, cwa_ref, cba_ref)) * conv(ub, cwb_ref, cbb_ref)


def _upconv_kernel(x_ref, g_ref, sh_ref, sc_ref, wa_ref, wb_ref, cwa_ref, cwb_ref, cba_ref, cbb_ref,
                   z_ref, st_ref, h_scr, *, tm, nj, tpb, nt):
    t = pl.program_id(0)
    n = _lag_rows(tm, nj)
    start = jnp.minimum(pl.program_id(1) * n, tm - n)
    wslot = t % 2
    zero_halo = jnp.zeros((HALO, h_scr.shape[2]), BF16)

    @pl.when(t == 0)
    def _():
        _norm_rows(h_scr, wslot, HALO, x_ref, g_ref, sc_ref, sh_ref, start, n)
        h_scr[wslot, 0:HALO, :] = zero_halo

    @pl.when(t > 0)
    def _():
        _norm_rows(h_scr, wslot, HALO, x_ref, g_ref, sc_ref, sh_ref, start, n)
        tail = h_scr[1 - wslot, tm:tm + HALO, :]
        h_scr[wslot, 0:HALO, :] = jnp.where(jnp.minimum(t, nt - 1) % tpb > 0, tail, zero_halo)
        h = h_scr[1 - wslot]

        def taps(w_ref):
            u = jnp.dot(h, w_ref[0], preferred_element_type=F32)
            return (pltpu.roll(u, 2, axis=0)[HALO:], pltpu.roll(u, 1, axis=0)[HALO:], u[HALO:])

        ua = taps(wa_ref)
        ub = taps(wb_ref)
        z_ref[0] = _conv_gate(ua, ub, cwa_ref, cwb_ref, cba_ref, cbb_ref).astype(z_ref.dtype)
        st_ref[0, 0, 0] = ua[2][tm - SUBLANES:, :]
        st_ref[0, 0, 1] = ub[2][tm - SUBLANES:, :]


def _halves_specs(shape, l, nj, col):
    return [pl.BlockSpec(shape, lambda *g: (l, 0, col(*g))), pl.BlockSpec(shape, lambda *g: (l, 0, col(*g) + nj))]


def upconv_prompt(x, g, l, mods, ks, wa, wb, conv_w, conv_b, *, tm=1024, tn=512):
    B, T, D = x.shape
    F = wa.shape[2]
    nj = F // tn
    L = conv_w.shape[0]
    tpb = T // tm
    nt = B * tpb
    cur = lambda t: jnp.minimum(t, nt - 1)
    prev = lambda t: jnp.maximum(t - 1, 0)
    colj = lambda t, j: jnp.where(t > 0, j, 0)
    ml, ksh, ksc = ks
    sh = mods.prompt(ml, ksh, D, batch=lambda t, j: cur(t) // tpb)
    sc = mods.prompt(ml, ksc, D, batch=lambda t, j: cur(t) // tpb)
    in_specs = [pl.BlockSpec((1, tm, D), lambda t, j: (cur(t) // tpb, cur(t) % tpb, 0)),
                pl.BlockSpec((1, 1, D), lambda t, j: (l, 0, 0)),
                sh[1], sc[1],
                pl.BlockSpec((1, D, tn), lambda t, j: (0, 0, colj(t, j))),
                pl.BlockSpec((1, D, tn), lambda t, j: (0, 0, colj(t, j)))]
    in_specs += _halves_specs((1, CONV_W, tn), l, nj, colj) + _halves_specs((1, 1, tn), l, nj, colj)
    cb = conv_b.reshape(L, 1, 2 * F)
    return pl.pallas_call(
        functools.partial(_upconv_kernel, tm=tm, nj=nj, tpb=tpb, nt=nt),
        grid=(nt + 1, nj),
        in_specs=in_specs,
        out_specs=[pl.BlockSpec((1, tm, tn), lambda t, j: (prev(t) // tpb, prev(t) % tpb, colj(t, j))),
                   pl.BlockSpec((1, 1, 2, SUBLANES, tn),
                                lambda t, j: (prev(t) // tpb, prev(t) % tpb, 0, 0, colj(t, j)))],
        out_shape=[jax.ShapeDtypeStruct((B, T, F), BF16),
                   jax.ShapeDtypeStruct((B, tpb, 2, SUBLANES, F), F32)],
        scratch_shapes=[pltpu.VMEM((2, HALO + tm, D), BF16)],
        compiler_params=_params(("arbitrary", "arbitrary")),
        name="upconv_prompt",
    )(x, g.reshape(L, 1, D), sh[0], sc[0], wa, wb, conv_w, conv_w, cb, cb)


def _upconv_step_kernel(x_ref, g_ref, sh_ref, sc_ref, wa_ref, wb_ref, cwa_ref, cwb_ref, cba_ref, cbb_ref,
                        sta_ref, stb_ref, z_ref, ua_ref, ub_ref, wao_ref, wbo_ref, h_scr, r_scr, *, rows, seq):
    @pl.when(pl.program_id(0) == 0)
    def _():
        _store_normed(h_scr, r_scr, 0, x_ref, g_ref, sc_ref, sh_ref, rows, seq)

    h = h_scr[...]
    nb = rows // seq

    def taps(w_ref, st_ref, u_ref, wo_ref):
        w = w_ref[0].astype(BF16)
        wo_ref[0] = w
        u = jnp.dot(h, w, preferred_element_type=F32)
        u_ref[...] = u
        u3 = u.reshape(nb, seq, u.shape[1])
        t = lax.broadcasted_iota(jnp.int32, u3.shape, 1)
        st0 = st_ref[0, :, 0:1, :]
        st1 = st_ref[0, :, 1:2, :]
        um1 = jnp.where(t == 0, st1, pltpu.roll(u3, 1, axis=1))
        um2 = jnp.where(t == 0, st0, jnp.where(t == 1, st1, pltpu.roll(u3, 2, axis=1)))
        return (um2, um1, u3)

    ua = taps(wa_ref, sta_ref, ua_ref, wao_ref)
    ub = taps(wb_ref, stb_ref, ub_ref, wbo_ref)
    z = _conv_gate(ua, ub, cwa_ref, cwb_ref, cba_ref, cbb_ref)
    z_ref[...] = z.reshape(rows, z.shape[2]).astype(z_ref.dtype)


def upconv_step(x, g, l, sh, sc, w_up, conv_w, conv_b, state, *, seq, tn=512):
    _, rows, D = x.shape
    F = w_up.shape[2] // 2
    nj = F // tn
    nb = rows // seq
    L = w_up.shape[0]
    col = lambda j: j
    in_specs = [pl.BlockSpec((1, rows, D), lambda j: (0, 0, 0)),
                pl.BlockSpec((1, 1, D), lambda j: (l, 0, 0)),
                sh[1], sc[1]]
    in_specs += _halves_specs((1, D, tn), l, nj, col) + _halves_specs((1, CONV_W, tn), l, nj, col)
    in_specs += _halves_specs((1, 1, tn), l, nj, col)
    in_specs += [pl.BlockSpec((1, nb, CONV_W - 1, tn), lambda j: (l, 0, 0, j)),
                 pl.BlockSpec((1, nb, CONV_W - 1, tn), lambda j: (l, 0, 0, j + nj))]
    cb = conv_b.reshape(L, 1, 2 * F)
    return pl.pallas_call(
        functools.partial(_upconv_step_kernel, rows=rows, seq=seq),
        grid=(nj,),
        in_specs=in_specs,
        out_specs=[pl.BlockSpec((rows, tn), lambda j: (0, j))] * 3 + [pl.BlockSpec((1, D, tn), lambda j: (0, 0, j))] * 2,
        out_shape=[jax.ShapeDtypeStruct((rows, F), BF16), jax.ShapeDtypeStruct((rows, F), F32),
                   jax.ShapeDtypeStruct((rows, F), F32), jax.ShapeDtypeStruct((1, D, F), BF16),
                   jax.ShapeDtypeStruct((1, D, F), BF16)],
        scratch_shapes=[pltpu.VMEM((rows, D), BF16), pltpu.VMEM((rows, 1), F32)],
        compiler_params=_params(("arbitrary",)),
        name="upconv_step",
    )(x, g.reshape(L, 1, D), sh[0], sc[0], w_up, w_up, conv_w, conv_w, cb, cb, state, state)


def _out_kernel(a_ref, w_ref, x_ref, gt_ref, o_ref, *wo_ref, tok):
    w = w_ref[0].astype(BF16)
    if wo_ref:
        wo_ref[0][0] = w
    y = jnp.dot(a_ref[0].astype(BF16), w, preferred_element_type=F32)
    if tok is None:
        o_ref[0] = x_ref[0] + gt_ref[0, 0] * y
    else:
        for b in range(y.shape[0] // tok):
            rows = slice(b * tok, (b + 1) * tok)
            o_ref[0, rows, :] = x_ref[0, rows, :] + gt_ref[0, b:b + 1, :] * y[rows]


def out_proj(a, w, l, x, gate, *, tm, tn, tok, emit_w=False):
    B, T, K = a.shape
    N = w.shape[2]
    assert not emit_w or (B == 1 and T == tm)
    out_specs = [pl.BlockSpec((1, tm, tn), lambda b, i, j: (b, i, j))]
    out_shape = [jax.ShapeDtypeStruct((B, T, N), F32)]
    if emit_w:
        out_specs.append(pl.BlockSpec((1, K, tn), lambda b, i, j: (0, 0, j)))
        out_shape.append(jax.ShapeDtypeStruct((1, K, N), BF16))
    outs = pl.pallas_call(
        functools.partial(_out_kernel, tok=tok),
        grid=(B, T // tm, N // tn),
        in_specs=[pl.BlockSpec((1, tm, K), lambda b, i, j: (b, i, 0)),
                  pl.BlockSpec((1, K, tn), lambda b, i, j: (l, 0, j)),
                  pl.BlockSpec((1, tm, tn), lambda b, i, j: (b, i, j)),
                  gate[1]],
        out_specs=out_specs,
        out_shape=out_shape,
        compiler_params=_params(("arbitrary", "arbitrary", "arbitrary")),
        name="out_proj",
    )(a, w, x, gate[0])
    return outs if emit_w else outs[0]


def _out_final_kernel(a_ref, w_ref, x_ref, gt_ref, nf_ref, o_ref):
    y = jnp.dot(a_ref[0], w_ref[0], preferred_element_type=F32)
    x = x_ref[0] + gt_ref[0, 0] * y
    ms = jnp.mean(x * x, axis=-1, keepdims=True)
    o_ref[0] = x * lax.rsqrt(ms + EPS) * nf_ref[...]


def out_proj_final(a, w, x, gate, norm_g, *, tm):
    B, T, K = a.shape
    N = w.shape[2]
    return pl.pallas_call(
        _out_final_kernel,
        grid=(B, T // tm),
        in_specs=[pl.BlockSpec((1, tm, K), lambda b, i: (b, i, 0)),
                  pl.BlockSpec((1, K, N), lambda b, i: (0, 0, 0), pipeline_mode=pl.Buffered(1)),
                  pl.BlockSpec((1, tm, N), lambda b, i: (b, i, 0)),
                  gate[1],
                  pl.BlockSpec((1, N), lambda b, i: (0, 0))],
        out_specs=pl.BlockSpec((1, tm, N), lambda b, i: (b, i, 0)),
        out_shape=jax.ShapeDtypeStruct((B, T, N), F32),
        compiler_params=_params(("arbitrary", "arbitrary")),
        name="out_proj_final",
    )(a, w, x, gate[0], norm_g.reshape(1, N))


def _rope_full(x, cos, sin):
    x1 = x[:, :LANES]
    x2 = x[:, LANES:]
    return jnp.concatenate([x1 * cos - x2 * sin, x2 * cos + x1 * sin], axis=1)


def _retention_head(q, k, v, gate, S, dec, qd, kd, gl, gn):
    qb = q.astype(BF16)
    vb = v.astype(BF16)
    scores = _bdot_nt(qb, k) * dec
    o = _bdot(scores, vb) + _bdot(qb, S) * qd
    kdt = (k * kd).T
    s_new = gl * S + _bdot(kdt, vb)
    mu = jnp.mean(o, axis=-1, keepdims=True)
    d = o - mu
    var = jnp.mean(d * d, axis=-1, keepdims=True)
    on = d * lax.rsqrt(var + EPS) * gn
    return _silu(gate) * on, s_new


def _ret_kernel(q_ref, k_ref, v_ref, g_ref, dec_ref, qd_ref, kd_ref, gl_ref, gn_ref, *rest, lsel):
    o_ref, s_ref = rest[-2:]

    @pl.when(pl.program_id(2) == 0)
    def _():
        s_ref[...] = jnp.zeros_like(s_ref)

    for hd in range(RET_HEADS_PER_STEP):
        kc = slice(hd * RET_DK, (hd + 1) * RET_DK)
        vc = slice(hd * RET_DV, (hd + 1) * RET_DV)
        o, s_new = _retention_head(q_ref[0, :, kc], k_ref[0, :, kc], v_ref[0, :, vc], g_ref[0, :, vc],
                                   s_ref[lsel, 0, hd], dec_ref[hd], qd_ref[hd], kd_ref[hd], gl_ref[hd],
                                   gn_ref[0, :, vc])
        s_ref[lsel, 0, hd] = s_new
        o_ref[0, :, vc] = o.astype(o_ref.dtype)


def _decay_tables(L, Lp):
    log_gamma = jnp.log(1.0 - jnp.exp2(-5.0 - jnp.arange(RET_HEADS, dtype=F32)))
    idx = jnp.arange(L, dtype=F32)
    rel = idx[:, None] - idx[None, :]
    dec = jnp.where(rel >= 0, jnp.exp(jnp.maximum(rel, 0.0)[None] * log_gamma[:, None, None]), 0.0)
    qd = jnp.exp((idx + 1.0)[None, :] * log_gamma[:, None])[..., None]
    kd = jnp.exp((L - 1.0 - idx)[None, :] * log_gamma[:, None])[..., None]
    gl = jnp.exp(L * log_gamma).reshape(RET_HEADS, 1, 1)
    p = Lp - L
    return (jnp.pad(dec, ((0, 0), (0, p), (0, p))), jnp.pad(qd, ((0, 0), (0, p), (0, 0))),
            jnp.pad(kd, ((0, 0), (0, p), (0, 0))), gl)


def _rope_tables_full(pos):
    half = RET_DK // 2
    inv = RET_ROPE_BASE ** (-jnp.arange(half, dtype=F32) * 2.0 / RET_DK)
    ang = pos.astype(F32)[:, None] * inv[None, :]
    return jnp.cos(ang), jnp.sin(ang)


def _chain(prev, in_specs, args, out_index):
    if prev is None:
        return {}
    in_specs.append(pl.BlockSpec(memory_space=pl.ANY))
    args.append(prev)
    return {len(args) - 1: out_index}


def _layer_block(prev, l):
    return (N_A, 0, l) if prev is None else (1, l, 0)


def retention_prompt(qkvg, gn, l, states):
    B, T, _ = qkvg.shape
    L = RET_BLOCK
    H = RET_HEADS
    P = RET_HEADS_PER_STEP
    dec, qd, kd, gl = _decay_tables(L, L)
    qk_blocks = H // P
    v0 = 2 * H * RET_DK // (P * RET_DV)
    in_specs = [pl.BlockSpec((1, L, P * RET_DK), lambda b, h, c: (b, c, h)),
                pl.BlockSpec((1, L, P * RET_DK), lambda b, h, c: (b, c, qk_blocks + h)),
                pl.BlockSpec((1, L, P * RET_DV), lambda b, h, c: (b, c, v0 + h)),
                pl.BlockSpec((1, L, P * RET_DV), lambda b, h, c: (b, c, v0 + qk_blocks + h)),
                pl.BlockSpec((P, L, L), lambda b, h, c: (h, 0, 0)),
                pl.BlockSpec((P, L, 1), lambda b, h, c: (h, 0, 0)),
                pl.BlockSpec((P, L, 1), lambda b, h, c: (h, 0, 0)),
                pl.BlockSpec((P, 1, 1), lambda b, h, c: (h, 0, 0)),
                pl.BlockSpec((1, 1, P * RET_DV), lambda b, h, c: (l, 0, h))]
    args = [qkvg, qkvg, qkvg, qkvg, dec, qd, kd, gl, gn.reshape(gn.shape[0], 1, -1)]
    aliases = _chain(states, in_specs, args, 1)
    nl, lb, lsel = _layer_block(states, l)
    return pl.pallas_call(
        functools.partial(_ret_kernel, lsel=lsel),
        grid=(B, H // P, T // L),
        in_specs=in_specs,
        out_specs=[pl.BlockSpec((1, L, P * RET_DV), lambda b, h, c: (b, c, h)),
                   pl.BlockSpec((nl, 1, P, RET_DK, RET_DV), lambda b, h, c: (lb, b, h, 0, 0))],
        out_shape=[jax.ShapeDtypeStruct((B, T, H * RET_DV), BF16),
                   jax.ShapeDtypeStruct((N_A, B, H, RET_DK, RET_DV), F32)],
        input_output_aliases=aliases,
        compiler_params=_params(("arbitrary", "arbitrary", "arbitrary")),
        name="retention_prompt",
    )(*args)


def _ret_step_kernel(x_ref, s_ref, dec_ref, qd_ref, kd_ref, gl_ref, gn_ref, *rest, seq, lsel):
    o_ref, so_ref = rest[-2:]
    H = RET_HEADS
    zpad = jnp.zeros((RET_PAD - seq, RET_DV), F32)
    for other in range(so_ref.shape[0]):
        if other != lsel:
            so_ref[other] = jnp.zeros(so_ref.shape[1:], F32)

    def padded(col0, width):
        return jnp.concatenate([x_ref[:, col0:col0 + width], zpad[:, :width]], axis=0)

    for h in range(H):
        q = padded(h * RET_DK, RET_DK)
        k = padded(H * RET_DK + h * RET_DK, RET_DK)
        v = padded(2 * H * RET_DK + h * RET_DV, RET_DV)
        gate = padded(2 * H * RET_DK + H * RET_DV + h * RET_DV, RET_DV)
        o, s_new = _retention_head(q, k, v, gate, s_ref[0, 0, h], dec_ref[h], qd_ref[h], kd_ref[h],
                                   gl_ref[h], gn_ref[0, :, h * RET_DV:(h + 1) * RET_DV])
        so_ref[lsel, 0, h] = s_new
        o_ref[:, h * RET_DV:(h + 1) * RET_DV] = o[:seq]


def retention_step(qkvg, state, gn, l, states, *, seq):
    rows, width = qkvg.shape
    nb = rows // seq
    H = RET_HEADS
    dec, qd, kd, gl = _decay_tables(seq, RET_PAD)
    whole = lambda shape: pl.BlockSpec(shape, lambda b: (0,) * len(shape))
    in_specs = [pl.BlockSpec((seq, width), lambda b: (b, 0)),
                pl.BlockSpec((1, 1, H, RET_DK, RET_DV), lambda b: (l, b, 0, 0, 0)),
                whole((H, RET_PAD, RET_PAD)), whole((H, RET_PAD, 1)), whole((H, RET_PAD, 1)), whole((H, 1, 1)),
                pl.BlockSpec((1, 1, H * RET_DV), lambda b: (l, 0, 0))]
    args = [qkvg, state, dec, qd, kd, gl, gn.reshape(gn.shape[0], 1, -1)]
    aliases = _chain(states, in_specs, args, 1)
    nl, lb, lsel = _layer_block(states, l)
    return pl.pallas_call(
        functools.partial(_ret_step_kernel, seq=seq, lsel=lsel),
        grid=(nb,),
        in_specs=in_specs,
        out_specs=[pl.BlockSpec((seq, H * RET_DV), lambda b: (b, 0)),
                   pl.BlockSpec((nl, 1, H, RET_DK, RET_DV), lambda b: (lb, b, 0, 0, 0))],
        out_shape=[jax.ShapeDtypeStruct((rows, H * RET_DV), F32),
                   jax.ShapeDtypeStruct(state.shape, F32)],
        input_output_aliases=aliases,
        compiler_params=_params(("arbitrary",)),
        name="retention_step",
    )(*args)


def _head_pairs(kcat, vcat):
    lo = lax.broadcasted_iota(jnp.int32, (kcat.shape[0], LANES), 1) < HEAD_DIM
    for pair in range(N_KV_HEADS // 2):
        kp = kcat[:, pair * LANES:(pair + 1) * LANES]
        vp = vcat[:, pair * LANES:(pair + 1) * LANES]
        kr = pltpu.roll(kp, HEAD_DIM, axis=1)
        vr = pltpu.roll(vp, HEAD_DIM, axis=1)
        for sub in range(2):
            ka, kb_ = (kp, kr) if sub == 0 else (kr, kp)
            va, vb_ = (vp, vr) if sub == 0 else (vr, vp)
            halves = [(jnp.where(lo, ka, 0.0).astype(BF16), jnp.where(lo, va, 0.0).astype(BF16)),
                      (jnp.where(lo, 0.0, kb_).astype(BF16), jnp.where(lo, 0.0, vb_).astype(BF16))]
            yield pair * 2 + sub, halves


def _sink_softmax(s, ok, sink):
    s = jnp.where(ok, s * (HEAD_DIM ** -0.5), NEG_INF)
    m = jnp.maximum(jnp.max(s, axis=-1, keepdims=True), sink)
    p = jnp.exp(s - m)
    return p * (1.0 / (jnp.sum(p, axis=-1, keepdims=True) + jnp.exp(sink - m)))


def _attend(q_of, kcat, vcat, ok, sink_ref, sink0, rows, store):
    pieces = GQA_GROUPS // 2
    for kh, halves in _head_pairs(kcat, vcat):
        base = kh * GQA_GROUPS * HEAD_DIM
        qs = jnp.concatenate([q_of(base + g * LANES) for g in range(pieces)], axis=0).astype(BF16)
        out = None
        for half, (kk, vv) in enumerate(halves):
            s = _bdot_nt(qs, kk)
            p = jnp.concatenate(
                [_sink_softmax(s[g * rows:(g + 1) * rows], ok, sink_ref[sink0 + kh * GQA_GROUPS + 2 * g + half])
                 for g in range(pieces)], axis=0)
            o = _bdot(p, vv)
            out = o if out is None else out + o
        for g in range(pieces):
            store(base + g * LANES, out[g * rows:(g + 1) * rows])


def _swa_kernel(sink_ref, q_ref, kp_ref, kc_ref, vp_ref, vc_ref, o_ref, *, sink0):
    i = pl.program_id(1)
    kcat = jnp.concatenate([kp_ref[0], kc_ref[0]], axis=0)
    vcat = jnp.concatenate([vp_ref[0], vc_ref[0]], axis=0)
    ql = lax.broadcasted_iota(jnp.int32, (ATT_BLOCK, 2 * ATT_BLOCK), 0)
    km = lax.broadcasted_iota(jnp.int32, (ATT_BLOCK, 2 * ATT_BLOCK), 1)
    no_prev = jnp.where(i > 0, 0, 4 * ATT_BLOCK)
    ok = ((km < ATT_BLOCK) & (km >= ql + no_prev)) | ((km >= ATT_BLOCK) & (km - ATT_BLOCK <= ql))

    def store(col0, val):
        o_ref[0, :, col0:col0 + LANES] = val.astype(o_ref.dtype)

    _attend(lambda c0: q_ref[0, :, c0:c0 + LANES], kcat, vcat, ok, sink_ref, sink0, ATT_BLOCK, store)


def swa_prompt(q, kv, sinks, j):
    B, T, DQ = q.shape
    DKV = kv.shape[2] // 2
    cur = lambda b, i: (b, i, 0)
    return pl.pallas_call(
        functools.partial(_swa_kernel, sink0=j * N_Q_HEADS),
        grid=(B, T // ATT_BLOCK),
        in_specs=[pl.BlockSpec(memory_space=pltpu.SMEM),
                  pl.BlockSpec((1, ATT_BLOCK, DQ), cur),
                  pl.BlockSpec((1, ATT_BLOCK, DKV), lambda b, i: (b, jnp.maximum(i - 1, 0), 0)),
                  pl.BlockSpec((1, ATT_BLOCK, DKV), lambda b, i: (b, i, 0)),
                  pl.BlockSpec((1, ATT_BLOCK, DKV), lambda b, i: (b, jnp.maximum(i - 1, 0), 1)),
                  pl.BlockSpec((1, ATT_BLOCK, DKV), lambda b, i: (b, i, 1))],
        out_specs=pl.BlockSpec((1, ATT_BLOCK, DQ), cur),
        out_shape=jax.ShapeDtypeStruct((B, T, DQ), BF16),
        compiler_params=_params(("arbitrary", "arbitrary")),
        name="swa_prompt",
    )(sinks, q, kv, kv, kv, kv)


def _swa_step_kernel(sink_ref, q_ref, ck_ref, nk_ref, cv_ref, nv_ref, o_ref, *, seq, wb, sink0):
    nk = 2 * ATT_BLOCK
    zpad = jnp.zeros((nk - wb - seq, nk_ref.shape[1]), F32)
    t = lax.broadcasted_iota(jnp.int32, (seq, nk), 0)
    s = lax.broadcasted_iota(jnp.int32, (seq, nk), 1)
    rel = t + wb - s
    ok = (rel >= 0) & (rel <= WINDOW)
    for e in range(ck_ref.shape[0]):
        rows = slice(e * seq, (e + 1) * seq)
        kcat = jnp.concatenate([ck_ref[e], nk_ref[rows, :], zpad], axis=0)
        vcat = jnp.concatenate([cv_ref[e], nv_ref[rows, :], zpad], axis=0)

        def store(col0, val, rows=rows):
            o_ref[rows, col0:col0 + LANES] = val

        _attend(lambda c0, rows=rows: q_ref[rows, c0:c0 + LANES], kcat, vcat, ok, sink_ref, sink0, seq, store)


def swa_step(q, cache_k, cache_v, kv_new, sinks, j, *, seq, per_step=4):
    rows, DQ = q.shape
    nb, wb, DKV = cache_k.shape
    E = per_step
    return pl.pallas_call(
        functools.partial(_swa_step_kernel, seq=seq, wb=wb, sink0=j * N_Q_HEADS),
        grid=(nb // E,),
        in_specs=[pl.BlockSpec(memory_space=pltpu.SMEM),
                  pl.BlockSpec((E * seq, DQ), lambda b: (b, 0)),
                  pl.BlockSpec((E, wb, DKV), lambda b: (b, 0, 0)), pl.BlockSpec((E * seq, DKV), lambda b: (b, 0)),
                  pl.BlockSpec((E, wb, DKV), lambda b: (b, 0, 0)), pl.BlockSpec((E * seq, DKV), lambda b: (b, 1))],
        out_specs=pl.BlockSpec((E * seq, DQ), lambda b: (b, 0)),
        out_shape=jax.ShapeDtypeStruct((rows, DQ), F32),
        compiler_params=_params(("arbitrary",)),
        name="swa_step",
    )(sinks, q, cache_k, kv_new, cache_v, kv_new)


def _rope_tables_partial(pos):
    half = ROPE_DIM // 2
    inv = ROPE_THETA ** (-jnp.arange(half, dtype=F32) * 2.0 / ROPE_DIM)
    ang = pos.astype(F32)[:, None] * inv[None, :]
    cos, sin = jnp.cos(ang), jnp.sin(ang)
    T = pos.shape[0]
    ones = jnp.ones((T, HEAD_DIM - ROPE_DIM), F32)
    zeros = jnp.zeros((T, HEAD_DIM - ROPE_DIM), F32)
    zh = jnp.zeros((T, half), F32)
    c = jnp.concatenate([cos, cos, ones], axis=1)
    s1 = jnp.concatenate([-sin, zh, zeros], axis=1)
    s2 = jnp.concatenate([zh, sin, zeros], axis=1)
    rep = LANES // HEAD_DIM
    return tuple(jnp.tile(a, (1, rep)) for a in (c, s1, s2))


def _final_norm_kernel(x_ref, g_ref, o_ref):
    x = x_ref[0]
    ms = jnp.mean(x * x, axis=-1, keepdims=True)
    o_ref[0] = x * lax.rsqrt(ms + EPS) * g_ref[...]


def final_norm(x, g, tm):
    B, T, D = x.shape
    return pl.pallas_call(
        _final_norm_kernel,
        grid=(B, T // tm),
        in_specs=[pl.BlockSpec((1, tm, D), lambda b, i: (b, i, 0)), pl.BlockSpec((1, D), lambda b, i: (0, 0))],
        out_specs=pl.BlockSpec((1, tm, D), lambda b, i: (b, i, 0)),
        out_shape=jax.ShapeDtypeStruct((B, T, D), F32),
        compiler_params=_params(("arbitrary", "arbitrary")),
        name="final_norm",
    )(x, g.reshape(1, D))


def kernel(x_prompt, x_sample, state_ret, cache_win_k, cache_win_v, state_conv, c_prompt, c_sample,
           w_ada, b_ada, norm_mix, norm_ffn, ret_w_in, ret_gn, ret_w_out,
           kv_norm, kv_w_ada, kv_b_ada, w_kv, att_w_q, att_sinks, att_w_o,
           ffn_w_up, ffn_conv_w, ffn_conv_b, ffn_w_down, norm_f):
    D = D_MODEL
    BP, TP, _ = x_prompt.shape
    BS, TS, _ = x_sample.shape
    RS = BS * TS
    KV = N_KV_HEADS * HEAD_DIM
    QK = RET_HEADS * RET_DK

    c_all = jnp.concatenate([c_sample, c_prompt], axis=0)
    c_all = jnp.pad(c_all, ((0, -c_all.shape[0] % PACK), (0, 0)))
    mods = Mods(ada_mods(c_all, w_ada, b_ada), BS)
    kv_mods = Mods(ada_mods(c_all, kv_w_ada[None], kv_b_ada[None]), BS)
    tile_col = lambda b, i, j: j

    pos_p = jnp.arange(TP, dtype=jnp.int32)
    pos_s = PAST_LEN + jnp.arange(TS, dtype=jnp.int32)
    tabs_p = _rope_tables_partial(pos_p)
    tabs_s = tuple(jnp.tile(a, (BS, 1)) for a in _rope_tables_partial(pos_s))
    full_p = _rope_tables_full(pos_p)
    full_s = tuple(jnp.tile(a, (BS, 1)) for a in _rope_tables_full(pos_s))
    sinks = att_sinks.reshape(-1)
    kv_norm1, w_kv1 = kv_norm[None], w_kv[None]

    x = x_sample.reshape(1, RS, D)
    wb = cache_win_k.shape[1]
    ck = cache_win_k.reshape(BS, wb, KV)
    cv = cache_win_v.reshape(BS, wb, KV)
    ret_s = None
    conv_s = []
    w16 = [dict() for _ in range(DEPTH)]
    for l in range(DEPTH):
        if l == N_A:
            kv_s, w16_kv = proj(x, kv_norm1, 0, kv_mods.sample(0, 0, D), kv_mods.sample(0, 1, D), w_kv1, 0, tm=RS,
                                tn=2 * KV, tok=TS, out_dtype=F32, rope="partial", rope_tabs=tabs_s, rope_cols=KV,
                                emit_w=True)
            kv_s = kv_s[0]
        sh1, sc1, sh2, sc2 = (mods.sample(l, k, D) for k in (0, 1, 3, 4))
        if l < N_A:
            qkvg, w16[l]["in"] = proj(x, norm_mix, l, sh1, sc1, ret_w_in, l, tm=RS, tn=SAMPLE_TN_PROJ, tok=TS,
                                      out_dtype=F32, rope="full", rope_tabs=full_s, rope_cols=2 * QK, scale_from=QK,
                                      emit_w=True)
            a, ret_s = retention_step(qkvg[0], state_ret, ret_gn, l, ret_s, seq=TS)
            w_mix, wl = ret_w_out, l
        else:
            j = l - N_A
            q, w16[l]["in"] = proj(x, norm_mix, l, sh1, sc1, att_w_q, j, tm=RS, tn=SAMPLE_TN_PROJ, tok=TS,
                                   out_dtype=F32, rope="partial", rope_tabs=tabs_s, rope_cols=D, emit_w=True)
            a = swa_step(q[0], ck, cv, kv_s, sinks, j, seq=TS)
            w_mix, wl = att_w_o, j
        tn = SAMPLE_TN_OUT[w_mix.shape[1]]
        x, w16[l]["out"] = out_proj(a[None], w_mix, wl, x, mods.sample(l, 2, tn, tile_col), tm=RS, tn=tn, tok=TS,
                                    emit_w=True)
        z, ua, ub, w16[l]["up_a"], w16[l]["up_b"] = upconv_step(x, norm_ffn, l, sh2, sc2, ffn_w_up, ffn_conv_w,
                                                                ffn_conv_b, state_conv, seq=TS, tn=SAMPLE_TN_UP)
        conv_s.append(jnp.concatenate([ua.reshape(BS, TS, D_FF)[:, TS - (CONV_W - 1):],
                                       ub.reshape(BS, TS, D_FF)[:, TS - (CONV_W - 1):]], axis=-1))
        tn = SAMPLE_TN_OUT[D_FF]
        x, w16[l]["down"] = out_proj(z[None], ffn_w_down, l, x, mods.sample(l, 5, tn, tile_col), tm=RS, tn=tn, tok=TS,
                                     emit_w=True)
    y_sample = final_norm(x, norm_f, RS).reshape(BS, TS, D)

    x = x_prompt
    ret_p = None
    conv_p = []
    for l in range(DEPTH):
        if l == N_A:
            kv_p = proj_prompt(x, kv_norm1, 0, kv_mods, (0, 0, 1), w16_kv, tm=PROMPT_TM, tn=2 * KV, out_dtype=F32,
                               rope="partial", rope_tabs=tabs_p, rope_cols=KV)
        if l < N_A:
            qkvg = proj_prompt(x, norm_mix, l, mods, (l, 0, 1), w16[l]["in"], tm=PROMPT_TM, tn=PROMPT_TN_PROJ,
                               out_dtype=F32, rope="full", rope_tabs=full_p, rope_cols=2 * QK, scale_from=QK)
            a, ret_p = retention_prompt(qkvg, ret_gn, l, ret_p)
        else:
            q = proj_prompt(x, norm_mix, l, mods, (l, 0, 1), w16[l]["in"], tm=PROMPT_TM, tn=PROMPT_TN_PROJ,
                            out_dtype=BF16, rope="partial", rope_tabs=tabs_p, rope_cols=D)
            a = swa_prompt(q, kv_p, sinks, l - N_A)
        x = out_proj(a, w16[l]["out"], 0, x, mods.prompt(l, 2, PROMPT_TN_OUT, tile_col), tm=PROMPT_TM,
                     tn=PROMPT_TN_OUT, tok=None)
        z, st = upconv_prompt(x, norm_ffn, l, mods, (l, 3, 4), w16[l]["up_a"], w16[l]["up_b"], ffn_conv_w, ffn_conv_b,
                              tm=PROMPT_TM, tn=PROMPT_TN_UP)
        conv_p.append(st[:, -1, :, SUBLANES - (CONV_W - 1):, :].transpose(0, 2, 1, 3).reshape(BP, CONV_W - 1, 2 * D_FF))
        if l < DEPTH - 1:
            x = out_proj(z, w16[l]["down"], 0, x, mods.prompt(l, 5, PROMPT_TN_OUT, tile_col), tm=PROMPT_TM,
                         tn=PROMPT_TN_OUT, tok=None)
        else:
            y_prompt = out_proj_final(z, w16[l]["down"], x, mods.prompt(l, 5, D), norm_f, tm=PROMPT_TM_FINAL)
    wp = min(WINDOW, TP)
    win_k_prompt = kv_p[:, -wp:, :KV].reshape(BP, wp, N_KV_HEADS, HEAD_DIM)
    win_v_prompt = kv_p[:, -wp:, KV:].reshape(BP, wp, N_KV_HEADS, HEAD_DIM)
    k_all = jnp.concatenate([ck, kv_s[:, :KV].reshape(BS, TS, KV)], axis=1)
    v_all = jnp.concatenate([cv, kv_s[:, KV:].reshape(BS, TS, KV)], axis=1)
    win_k_sample = k_all[:, -wb:].reshape(BS, wb, N_KV_HEADS, HEAD_DIM)
    win_v_sample = v_all[:, -wb:].reshape(BS, wb, N_KV_HEADS, HEAD_DIM)

    return (y_prompt, y_sample, ret_p, ret_s, win_k_prompt, win_v_prompt,
            win_k_sample, win_v_sample, jnp.stack(conv_p), jnp.stack(conv_s))
```

```python
import functools

import jax
import jax.numpy as jnp
from jax import lax
from jax.experimental import pallas as pl
from jax.experimental.pallas import tpu as pltpu

F32 = jnp.float32
BF16 = jnp.bfloat16

D_MODEL = 2048
DEPTH = 4
PAST_LEN = 16384
N_A = DEPTH // 2
RET_HEADS = 8
RET_DK = D_MODEL // RET_HEADS
RET_DV = 2 * D_MODEL // RET_HEADS
RET_ROPE_BASE = 10000.0
N_Q_HEADS = 32
N_KV_HEADS = 4
HEAD_DIM = D_MODEL // N_Q_HEADS
GQA_GROUPS = N_Q_HEADS // N_KV_HEADS
ROPE_DIM = HEAD_DIM // 4
ROPE_THETA = 500000.0
WINDOW = 128
ATT_BLOCK = 128
D_FF = 2 * D_MODEL
CONV_W = 3
N_MOD = 6
EPS = 1e-6
NEG_INF = -1e30

LANES = 128
SUBLANES = 8
PACK = 16
HALO = PACK
RET_BLOCK = 512
RET_HEADS_PER_STEP = 4
RET_PAD = PACK
VMEM_LIMIT = 56 * 1024 * 1024
PROMPT_TM = 1024
PROMPT_TN_PROJ = 1024
PROMPT_TN_OUT = 1024
PROMPT_TN_UP = 512
PROMPT_TM_FINAL = 512
SAMPLE_TN_PROJ = 1024
SAMPLE_TN_UP = 512
SAMPLE_TN_OUT = {RET_HEADS * RET_DV: 512, D_MODEL: 1024}


def _params(sem):
    return pltpu.CompilerParams(dimension_semantics=sem, vmem_limit_bytes=VMEM_LIMIT)


def _silu(x):
    return x * jax.nn.sigmoid(x)


def _bdot(a, b):
    return jnp.dot(a.astype(BF16), b.astype(BF16), preferred_element_type=F32)


def _bdot_nt(a, b):
    return lax.dot_general(a.astype(BF16), b.astype(BF16), (((1,), (1,)), ((), ())), preferred_element_type=F32)


def _norm_mod(x, g, sc, sh):
    ms = jnp.mean(x * x, axis=-1, keepdims=True)
    y = x * lax.rsqrt(ms + EPS) * g
    return y * (1.0 + sc) + sh


class Mods:
    def __init__(self, arr, n_sample):
        self.a3 = arr
        self.a4 = arr.reshape(arr.shape[0], arr.shape[1], 1, arr.shape[2])
        self.n_sample = n_sample

    def prompt(self, l, k, width, col=lambda *g: 0, batch=lambda *g: g[0]):
        per = D_MODEL // width
        off = self.n_sample
        return self.a4, pl.BlockSpec((1, 1, 1, width), lambda *g: (l, off + batch(*g), 0, k * per + col(*g)))

    def sample(self, l, k, width, col=lambda *g: 0):
        per = D_MODEL // width
        return self.a3, pl.BlockSpec((1, self.n_sample, width), lambda *g: (l, 0, k * per + col(*g)))


def _ada_kernel(c_ref, w_ref, b_ref, o_ref):
    o_ref[0] = _bdot(_silu(c_ref[...]), w_ref[0]) + b_ref[0]


def ada_mods(c_all, w, b, tn=1024):
    L, D, N = w.shape
    R = c_all.shape[0]
    return pl.pallas_call(
        _ada_kernel,
        grid=(L, N // tn),
        in_specs=[pl.BlockSpec((R, D), lambda l, j: (0, 0)),
                  pl.BlockSpec((1, D, tn), lambda l, j: (l, 0, j)),
                  pl.BlockSpec((1, 1, tn), lambda l, j: (l, 0, j))],
        out_specs=pl.BlockSpec((1, R, tn), lambda l, j: (l, 0, j)),
        out_shape=jax.ShapeDtypeStruct((L, R, N), F32),
        compiler_params=_params(("arbitrary", "arbitrary")),
        name="ada_mods",
    )(c_all, w, b.reshape(L, 1, N))


def _store_normed(h_scr, r_scr, row0, x_ref, g_ref, sc_ref, sh_ref, rows, tok):
    def stats(r, carry):
        r0 = pl.multiple_of(r * PACK, PACK)
        x = x_ref[0, pl.ds(r0, PACK), :]
        r_scr[pl.ds(r0, PACK), :] = lax.rsqrt(jnp.mean(x * x, axis=-1, keepdims=True) + EPS)
        return carry

    lax.fori_loop(0, rows // PACK, stats, 0, unroll=4)
    g = g_ref[0]

    def piece(start, n, sc, sh):
        return x_ref[0, pl.ds(start, n), :] * r_scr[pl.ds(start, n), :] * g * (1.0 + sc) + sh

    def apply(r, carry):
        r0 = pl.multiple_of(r * PACK, PACK)
        if tok is None:
            h = piece(r0, PACK, sc_ref[0, 0], sh_ref[0, 0])
        else:
            parts = []
            for s in range(PACK // tok):
                m = r * (PACK // tok) + s
                start = pl.multiple_of(r0 + s * tok, tok)
                parts.append(piece(start, tok, sc_ref[0, pl.ds(m, 1), :], sh_ref[0, pl.ds(m, 1), :]))
            h = jnp.concatenate(parts, axis=0)
        h_scr[pl.ds(pl.multiple_of(row0 + r0, PACK), PACK), :] = h.astype(BF16)
        return carry

    lax.fori_loop(0, rows // PACK, apply, 0, unroll=2)


def _rope64(a, c, s1, s2):
    return a * c + pltpu.roll(a, LANES - ROPE_DIM // 2, axis=1) * s1 + pltpu.roll(a, ROPE_DIM // 2, axis=1) * s2


def _proj_kernel(*refs, tm, tn, tok, rope, rope_cols, scale_from, emit_w):
    x_ref, g_ref, sh_ref, sc_ref, w_ref = refs[:5]
    if rope:
        t1_ref, t2_ref, t3_ref = refs[5:8]
    o_ref = refs[-4] if emit_w else refs[-3]
    h_scr, r_scr = refs[-2:]
    j = pl.program_id(2)

    @pl.when(j == 0)
    def _():
        _store_normed(h_scr, r_scr, 0, x_ref, g_ref, sc_ref, sh_ref, tm, tok)

    wb = w_ref[0].astype(BF16)
    if emit_w:
        refs[-3][0] = wb
    acc = jnp.dot(h_scr[...], wb, preferred_element_type=F32)
    _proj_epilogue(acc, o_ref, j, tn, rope, rope_cols, scale_from, refs[5:8])


def _proj_epilogue(acc, o_ref, j, tn, rope, rope_cols, scale_from, tabs):
    if rope is None:
        o_ref[0] = acc.astype(o_ref.dtype)
    elif rope == "partial":
        t1_ref, t2_ref, t3_ref = tabs
        for c in range(tn // LANES):
            a = acc[:, c * LANES:(c + 1) * LANES]
            if c * LANES < rope_cols:
                a = _rope64(a, t1_ref[...], t2_ref[...], t3_ref[...])
            o_ref[0, :, c * LANES:(c + 1) * LANES] = a.astype(o_ref.dtype)
    else:
        t1_ref, t2_ref, _ = tabs
        rotated = j * tn < rope_cols
        cos = jnp.where(rotated, t1_ref[...], 1.0)
        sin = jnp.where(rotated, t2_ref[...], 0.0)
        scale = jnp.where(rotated & (j * tn >= scale_from), RET_DK ** -0.5, 1.0).astype(F32)
        for hd in range(tn // RET_DK):
            x1 = acc[:, hd * RET_DK:hd * RET_DK + LANES]
            x2 = acc[:, hd * RET_DK + LANES:(hd + 1) * RET_DK]
            o_ref[0, :, hd * RET_DK:hd * RET_DK + LANES] = ((x1 * cos - x2 * sin) * scale).astype(o_ref.dtype)
            o_ref[0, :, hd * RET_DK + LANES:(hd + 1) * RET_DK] = ((x2 * cos + x1 * sin) * scale).astype(o_ref.dtype)


def _norm_rows(h_scr, slot, row0, x_ref, g_ref, sc_ref, sh_ref, start, n):
    g = g_ref[0]
    sc1 = 1.0 + sc_ref[0, 0]
    sh = sh_ref[0, 0]
    for c in range(n // PACK):
        r0 = pl.multiple_of(start + c * PACK, PACK)
        x = x_ref[0, pl.ds(r0, PACK), :]
        r = lax.rsqrt(jnp.mean(x * x, axis=-1, keepdims=True) + EPS)
        h_scr[slot, pl.ds(pl.multiple_of(row0 + r0, PACK), PACK), :] = (x * r * g * sc1 + sh).astype(BF16)


def _lag_rows(tm, nj):
    return PACK * -(-tm // (PACK * nj))


def _proj_lag_kernel(*refs, tm, tn, nj, rope, rope_cols, scale_from):
    x_ref, g_ref, sh_ref, sc_ref, w_ref = refs[:5]
    o_ref, h_scr = refs[-2:]
    t = pl.program_id(0)
    j = pl.program_id(1)
    n = _lag_rows(tm, nj)
    start = jnp.minimum(j * n, tm - n)
    wslot = t % 2

    @pl.when(t == 0)
    def _():
        _norm_rows(h_scr, wslot, 0, x_ref, g_ref, sc_ref, sh_ref, start, n)

    @pl.when(t > 0)
    def _():
        _norm_rows(h_scr, wslot, 0, x_ref, g_ref, sc_ref, sh_ref, start, n)
        acc = jnp.dot(h_scr[1 - wslot], w_ref[0], preferred_element_type=F32)
        _proj_epilogue(acc, o_ref, j, tn, rope, rope_cols, scale_from, refs[5:8])


def proj_prompt(x, g, l, mods, ks, w, *, tm, tn, out_dtype, rope=None, rope_tabs=None, rope_cols=0, scale_from=0):
    B, T, D = x.shape
    N = w.shape[2]
    assert rope != "partial" or rope_cols == N or tn == N
    tpb = T // tm
    nt = B * tpb
    nj = N // tn
    cur = lambda t: jnp.minimum(t, nt - 1)
    prev = lambda t: jnp.maximum(t - 1, 0)
    colj = lambda t, j: jnp.where(t > 0, j, 0)
    ml, ksh, ksc = ks
    sh = mods.prompt(ml, ksh, D, batch=lambda t, j: cur(t) // tpb)
    sc = mods.prompt(ml, ksc, D, batch=lambda t, j: cur(t) // tpb)
    in_specs = [pl.BlockSpec((1, tm, D), lambda t, j: (cur(t) // tpb, cur(t) % tpb, 0)),
                pl.BlockSpec((1, 1, D), lambda t, j: (l, 0, 0)),
                sh[1], sc[1],
                pl.BlockSpec((1, D, tn), lambda t, j: (0, 0, colj(t, j)))]
    args = [x, g.reshape(g.shape[0], 1, D), sh[0], sc[0], w]
    if rope:
        tabs = list(rope_tabs) + ([rope_tabs[0]] if len(rope_tabs) == 2 else [])
        in_specs += [pl.BlockSpec((tm, LANES), lambda t, j: (prev(t) % tpb, 0))] * 3
        args += tabs
    return pl.pallas_call(
        functools.partial(_proj_lag_kernel, tm=tm, tn=tn, nj=nj, rope=rope, rope_cols=rope_cols, scale_from=scale_from),
        grid=(nt + 1, nj),
        in_specs=in_specs,
        out_specs=pl.BlockSpec((1, tm, tn), lambda t, j: (prev(t) // tpb, prev(t) % tpb, colj(t, j))),
        out_shape=jax.ShapeDtypeStruct((B, T, N), out_dtype),
        scratch_shapes=[pltpu.VMEM((2, tm, D), BF16)],
        compiler_params=_params(("arbitrary", "arbitrary")),
        name="proj_prompt",
    )(*args)


def proj(x, g, l, sh, sc, w, wl, *, tm, tn, tok, out_dtype, rope=None, rope_tabs=None, rope_cols=0, scale_from=0,
         emit_w=False):
    B, T, D = x.shape
    N = w.shape[2]
    assert rope != "partial" or rope_cols == N or tn == N
    assert not emit_w or (B == 1 and T == tm)
    out_specs = [pl.BlockSpec((1, tm, tn), lambda b, i, j: (b, i, j))]
    out_shape = [jax.ShapeDtypeStruct((B, T, N), out_dtype)]
    if emit_w:
        out_specs.append(pl.BlockSpec((1, D, tn), lambda b, i, j: (0, 0, j)))
        out_shape.append(jax.ShapeDtypeStruct((1, D, N), BF16))
    in_specs = [pl.BlockSpec((1, tm, D), lambda b, i, j: (b, i, 0)),
                pl.BlockSpec((1, 1, D), lambda b, i, j: (l, 0, 0)),
                sh[1], sc[1],
                pl.BlockSpec((1, D, tn), lambda b, i, j: (wl, 0, j))]
    args = [x, g.reshape(g.shape[0], 1, D), sh[0], sc[0], w]
    if rope:
        tabs = list(rope_tabs) + ([rope_tabs[0]] if len(rope_tabs) == 2 else [])
        in_specs += [pl.BlockSpec((tm, LANES), lambda b, i, j: (i, 0))] * 3
        args += tabs
    outs = pl.pallas_call(
        functools.partial(_proj_kernel, tm=tm, tn=tn, tok=tok, rope=rope, rope_cols=rope_cols, scale_from=scale_from,
                          emit_w=emit_w),
        grid=(B, T // tm, N // tn),
        in_specs=in_specs,
        out_specs=out_specs,
        out_shape=out_shape,
        scratch_shapes=[pltpu.VMEM((tm, D), BF16), pltpu.VMEM((tm, 1), F32)],
        compiler_params=_params(("arbitrary", "arbitrary", "arbitrary")),
        name="proj",
    )(*args)
    return outs if emit_w else outs[0]


def _conv_gate(ua, ub, cwa_ref, cwb_ref, cba_ref, cbb_ref):
    def conv(u, cw_ref, cb_ref):
        acc = cb_ref[0] + cw_ref[0, 0:1, :] * u[0]
        acc = acc + cw_ref[0, 1:2, :] * u[1]
        return acc + cw_ref[0, 2:3, :] * u[2]

    return _silu(conv(ua, cwa_ref, cba_ref)) * conv(ub, cwb_ref, cbb_ref)


def _upconv_kernel(x_ref, g_ref, sh_ref, sc_ref, wa_ref, wb_ref, cwa_ref, cwb_ref, cba_ref, cbb_ref,
                   z_ref, st_ref, h_scr, *, tm, nj, tpb, nt):
    t = pl.program_id(0)
    n = _lag_rows(tm, nj)
    start = jnp.minimum(pl.program_id(1) * n, tm - n)
    wslot = t % 2
    zero_halo = jnp.zeros((HALO, h_scr.shape[2]), BF16)

    @pl.when(t == 0)
    def _():
        _norm_rows(h_scr, wslot, HALO, x_ref, g_ref, sc_ref, sh_ref, start, n)
        h_scr[wslot, 0:HALO, :] = zero_halo

    @pl.when(t > 0)
    def _():
        _norm_rows(h_scr, wslot, HALO, x_ref, g_ref, sc_ref, sh_ref, start, n)
        tail = h_scr[1 - wslot, tm:tm + HALO, :]
        h_scr[wslot, 0:HALO, :] = jnp.where(jnp.minimum(t, nt - 1) % tpb > 0, tail, zero_halo)
        h = h_scr[1 - wslot]

        def taps(w_ref):
            u = jnp.dot(h, w_ref[0], preferred_element_type=F32)
            return (pltpu.roll(u, 2, axis=0)[HALO:], pltpu.roll(u, 1, axis=0)[HALO:], u[HALO:])

        ua = taps(wa_ref)
        ub = taps(wb_ref)
        z_ref[0] = _conv_gate(ua, ub, cwa_ref, cwb_ref, cba_ref, cbb_ref).astype(z_ref.dtype)
        st_ref[0, 0, 0] = ua[2][tm - SUBLANES:, :]
        st_ref[0, 0, 1] = ub[2][tm - SUBLANES:, :]


def _halves_specs(shape, l, nj, col):
    return [pl.BlockSpec(shape, lambda *g: (l, 0, col(*g))), pl.BlockSpec(shape, lambda *g: (l, 0, col(*g) + nj))]


def upconv_prompt(x, g, l, mods, ks, wa, wb, conv_w, conv_b, *, tm=1024, tn=512):
    B, T, D = x.shape
    F = wa.shape[2]
    nj = F // tn
    L = conv_w.shape[0]
    tpb = T // tm
    nt = B * tpb
    cur = lambda t: jnp.minimum(t, nt - 1)
    prev = lambda t: jnp.maximum(t - 1, 0)
    colj = lambda t, j: jnp.where(t > 0, j, 0)
    ml, ksh, ksc = ks
    sh = mods.prompt(ml, ksh, D, batch=lambda t, j: cur(t) // tpb)
    sc = mods.prompt(ml, ksc, D, batch=lambda t, j: cur(t) // tpb)
    in_specs = [pl.BlockSpec((1, tm, D), lambda t, j: (cur(t) // tpb, cur(t) % tpb, 0)),
                pl.BlockSpec((1, 1, D), lambda t, j: (l, 0, 0)),
                sh[1], sc[1],
                pl.BlockSpec((1, D, tn), lambda t, j: (0, 0, colj(t, j))),
                pl.BlockSpec((1, D, tn), lambda t, j: (0, 0, colj(t, j)))]
    in_specs += _halves_specs((1, CONV_W, tn), l, nj, colj) + _halves_specs((1, 1, tn), l, nj, colj)
    cb = conv_b.reshape(L, 1, 2 * F)
    return pl.pallas_call(
        functools.partial(_upconv_kernel, tm=tm, nj=nj, tpb=tpb, nt=nt),
        grid=(nt + 1, nj),
        in_specs=in_specs,
        out_specs=[pl.BlockSpec((1, tm, tn), lambda t, j: (prev(t) // tpb, prev(t) % tpb, colj(t, j))),
                   pl.BlockSpec((1, 1, 2, SUBLANES, tn),
                                lambda t, j: (prev(t) // tpb, prev(t) % tpb, 0, 0, colj(t, j)))],
        out_shape=[jax.ShapeDtypeStruct((B, T, F), BF16),
                   jax.ShapeDtypeStruct((B, tpb, 2, SUBLANES, F), F32)],
        scratch_shapes=[pltpu.VMEM((2, HALO + tm, D), BF16)],
        compiler_params=_params(("arbitrary", "arbitrary")),
        name="upconv_prompt",
    )(x, g.reshape(L, 1, D), sh[0], sc[0], wa, wb, conv_w, conv_w, cb, cb)


def _upconv_step_kernel(x_ref, g_ref, sh_ref, sc_ref, wa_ref, wb_ref, cwa_ref, cwb_ref, cba_ref, cbb_ref,
                        sta_ref, stb_ref, z_ref, ua_ref, ub_ref, wao_ref, wbo_ref, h_scr, r_scr, *, rows, seq):
    @pl.when(pl.program_id(0) == 0)
    def _():
        _store_normed(h_scr, r_scr, 0, x_ref, g_ref, sc_ref, sh_ref, rows, seq)

    h = h_scr[...]
    nb = rows // seq

    def taps(w_ref, st_ref, u_ref, wo_ref):
        w = w_ref[0].astype(BF16)
        wo_ref[0] = w
        u = jnp.dot(h, w, preferred_element_type=F32)
        u_ref[...] = u
        u3 = u.reshape(nb, seq, u.shape[1])
        t = lax.broadcasted_iota(jnp.int32, u3.shape, 1)
        st0 = st_ref[0, :, 0:1, :]
        st1 = st_ref[0, :, 1:2, :]
        um1 = jnp.where(t == 0, st1, pltpu.roll(u3, 1, axis=1))
        um2 = jnp.where(t == 0, st0, jnp.where(t == 1, st1, pltpu.roll(u3, 2, axis=1)))
        return (um2, um1, u3)

    ua = taps(wa_ref, sta_ref, ua_ref, wao_ref)
    ub = taps(wb_ref, stb_ref, ub_ref, wbo_ref)
    z = _conv_gate(ua, ub, cwa_ref, cwb_ref, cba_ref, cbb_ref)
    z_ref[...] = z.reshape(rows, z.shape[2]).astype(z_ref.dtype)


def upconv_step(x, g, l, sh, sc, w_up, conv_w, conv_b, state, *, seq, tn=512):
    _, rows, D = x.shape
    F = w_up.shape[2] // 2
    nj = F // tn
    nb = rows // seq
    L = w_up.shape[0]
    col = lambda j: j
    in_specs = [pl.BlockSpec((1, rows, D), lambda j: (0, 0, 0)),
                pl.BlockSpec((1, 1, D), lambda j: (l, 0, 0)),
                sh[1], sc[1]]
    in_specs += _halves_specs((1, D, tn), l, nj, col) + _halves_specs((1, CONV_W, tn), l, nj, col)
    in_specs += _halves_specs((1, 1, tn), l, nj, col)
    in_specs += [pl.BlockSpec((1, nb, CONV_W - 1, tn), lambda j: (l, 0, 0, j)),
                 pl.BlockSpec((1, nb, CONV_W - 1, tn), lambda j: (l, 0, 0, j + nj))]
    cb = conv_b.reshape(L, 1, 2 * F)
    return pl.pallas_call(
        functools.partial(_upconv_step_kernel, rows=rows, seq=seq),
        grid=(nj,),
        in_specs=in_specs,
        out_specs=[pl.BlockSpec((rows, tn), lambda j: (0, j))] * 3 + [pl.BlockSpec((1, D, tn), lambda j: (0, 0, j))] * 2,
        out_shape=[jax.ShapeDtypeStruct((rows, F), BF16), jax.ShapeDtypeStruct((rows, F), F32),
                   jax.ShapeDtypeStruct((rows, F), F32), jax.ShapeDtypeStruct((1, D, F), BF16),
                   jax.ShapeDtypeStruct((1, D, F), BF16)],
        scratch_shapes=[pltpu.VMEM((rows, D), BF16), pltpu.VMEM((rows, 1), F32)],
        compiler_params=_params(("arbitrary",)),
        name="upconv_step",
    )(x, g.reshape(L, 1, D), sh[0], sc[0], w_up, w_up, conv_w, conv_w, cb, cb, state, state)


def _out_kernel(a_ref, w_ref, x_ref, gt_ref, o_ref, *wo_ref, tok):
    w = w_ref[0].astype(BF16)
    if wo_ref:
        wo_ref[0][0] = w
    y = jnp.dot(a_ref[0].astype(BF16), w, preferred_element_type=F32)
    if tok is None:
        o_ref[0] = x_ref[0] + gt_ref[0, 0] * y
    else:
        for b in range(y.shape[0] // tok):
            rows = slice(b * tok, (b + 1) * tok)
            o_ref[0, rows, :] = x_ref[0, rows, :] + gt_ref[0, b:b + 1, :] * y[rows]


def out_proj(a, w, l, x, gate, *, tm, tn, tok, emit_w=False):
    B, T, K = a.shape
    N = w.shape[2]
    assert not emit_w or (B == 1 and T == tm)
    out_specs = [pl.BlockSpec((1, tm, tn), lambda b, i, j: (b, i, j))]
    out_shape = [jax.ShapeDtypeStruct((B, T, N), F32)]
    if emit_w:
        out_specs.append(pl.BlockSpec((1, K, tn), lambda b, i, j: (0, 0, j)))
        out_shape.append(jax.ShapeDtypeStruct((1, K, N), BF16))
    outs = pl.pallas_call(
        functools.partial(_out_kernel, tok=tok),
        grid=(B, T // tm, N // tn),
        in_specs=[pl.BlockSpec((1, tm, K), lambda b, i, j: (b, i, 0)),
                  pl.BlockSpec((1, K, tn), lambda b, i, j: (l, 0, j)),
                  pl.BlockSpec((1, tm, tn), lambda b, i, j: (b, i, j)),
                  gate[1]],
        out_specs=out_specs,
        out_shape=out_shape,
        compiler_params=_params(("arbitrary", "arbitrary", "arbitrary")),
        name="out_proj",
    )(a, w, x, gate[0])
    return outs if emit_w else outs[0]


def _out_final_kernel(a_ref, w_ref, x_ref, gt_ref, nf_ref, o_ref):
    y = jnp.dot(a_ref[0], w_ref[0], preferred_element_type=F32)
    x = x_ref[0] + gt_ref[0, 0] * y
    ms = jnp.mean(x * x, axis=-1, keepdims=True)
    o_ref[0] = x * lax.rsqrt(ms + EPS) * nf_ref[...]


def out_proj_final(a, w, x, gate, norm_g, *, tm):
    B, T, K = a.shape
    N = w.shape[2]
    return pl.pallas_call(
        _out_final_kernel,
        grid=(B, T // tm),
        in_specs=[pl.BlockSpec((1, tm, K), lambda b, i: (b, i, 0)),
                  pl.BlockSpec((1, K, N), lambda b, i: (0, 0, 0), pipeline_mode=pl.Buffered(1)),
                  pl.BlockSpec((1, tm, N), lambda b, i: (b, i, 0)),
                  gate[1],
                  pl.BlockSpec((1, N), lambda b, i: (0, 0))],
        out_specs=pl.BlockSpec((1, tm, N), lambda b, i: (b, i, 0)),
        out_shape=jax.ShapeDtypeStruct((B, T, N), F32),
        compiler_params=_params(("arbitrary", "arbitrary")),
        name="out_proj_final",
    )(a, w, x, gate[0], norm_g.reshape(1, N))


def _rope_full(x, cos, sin):
    x1 = x[:, :LANES]
    x2 = x[:, LANES:]
    return jnp.concatenate([x1 * cos - x2 * sin, x2 * cos + x1 * sin], axis=1)


def _retention_head(q, k, v, gate, S, dec, qd, kd, gl, gn):
    qb = q.astype(BF16)
    vb = v.astype(BF16)
    scores = _bdot_nt(qb, k) * dec
    o = _bdot(scores, vb) + _bdot(qb, S) * qd
    kdt = (k * kd).T
    s_new = gl * S + _bdot(kdt, vb)
    mu = jnp.mean(o, axis=-1, keepdims=True)
    d = o - mu
    var = jnp.mean(d * d, axis=-1, keepdims=True)
    on = d * lax.rsqrt(var + EPS) * gn
    return _silu(gate) * on, s_new


def _ret_kernel(q_ref, k_ref, v_ref, g_ref, dec_ref, qd_ref, kd_ref, gl_ref, gn_ref, *rest, lsel):
    o_ref, s_ref = rest[-2:]

    @pl.when(pl.program_id(2) == 0)
    def _():
        s_ref[...] = jnp.zeros_like(s_ref)

    for hd in range(RET_HEADS_PER_STEP):
        kc = slice(hd * RET_DK, (hd + 1) * RET_DK)
        vc = slice(hd * RET_DV, (hd + 1) * RET_DV)
        o, s_new = _retention_head(q_ref[0, :, kc], k_ref[0, :, kc], v_ref[0, :, vc], g_ref[0, :, vc],
                                   s_ref[lsel, 0, hd], dec_ref[hd], qd_ref[hd], kd_ref[hd], gl_ref[hd],
                                   gn_ref[0, :, vc])
        s_ref[lsel, 0, hd] = s_new
        o_ref[0, :, vc] = o.astype(o_ref.dtype)


def _decay_tables(L, Lp):
    log_gamma = jnp.log(1.0 - jnp.exp2(-5.0 - jnp.arange(RET_HEADS, dtype=F32)))
    idx = jnp.arange(L, dtype=F32)
    rel = idx[:, None] - idx[None, :]
    dec = jnp.where(rel >= 0, jnp.exp(jnp.maximum(rel, 0.0)[None] * log_gamma[:, None, None]), 0.0)
    qd = jnp.exp((idx + 1.0)[None, :] * log_gamma[:, None])[..., None]
    kd = jnp.exp((L - 1.0 - idx)[None, :] * log_gamma[:, None])[..., None]
    gl = jnp.exp(L * log_gamma).reshape(RET_HEADS, 1, 1)
    p = Lp - L
    return (jnp.pad(dec, ((0, 0), (0, p), (0, p))), jnp.pad(qd, ((0, 0), (0, p), (0, 0))),
            jnp.pad(kd, ((0, 0), (0, p), (0, 0))), gl)


def _rope_tables_full(pos):
    half = RET_DK // 2
    inv = RET_ROPE_BASE ** (-jnp.arange(half, dtype=F32) * 2.0 / RET_DK)
    ang = pos.astype(F32)[:, None] * inv[None, :]
    return jnp.cos(ang), jnp.sin(ang)


def _chain(prev, in_specs, args, out_index):
    if prev is None:
        return {}
    in_specs.append(pl.BlockSpec(memory_space=pl.ANY))
    args.append(prev)
    return {len(args) - 1: out_index}


def _layer_block(prev, l):
    return (N_A, 0, l) if prev is None else (1, l, 0)


def retention_prompt(qkvg, gn, l, states):
    B, T, _ = qkvg.shape
    L = RET_BLOCK
    H = RET_HEADS
    P = RET_HEADS_PER_STEP
    dec, qd, kd, gl = _decay_tables(L, L)
    qk_blocks = H // P
    v0 = 2 * H * RET_DK // (P * RET_DV)
    in_specs = [pl.BlockSpec((1, L, P * RET_DK), lambda b, h, c: (b, c, h)),
                pl.BlockSpec((1, L, P * RET_DK), lambda b, h, c: (b, c, qk_blocks + h)),
                pl.BlockSpec((1, L, P * RET_DV), lambda b, h, c: (b, c, v0 + h)),
                pl.BlockSpec((1, L, P * RET_DV), lambda b, h, c: (b, c, v0 + qk_blocks + h)),
                pl.BlockSpec((P, L, L), lambda b, h, c: (h, 0, 0)),
                pl.BlockSpec((P, L, 1), lambda b, h, c: (h, 0, 0)),
                pl.BlockSpec((P, L, 1), lambda b, h, c: (h, 0, 0)),
                pl.BlockSpec((P, 1, 1), lambda b, h, c: (h, 0, 0)),
                pl.BlockSpec((1, 1, P * RET_DV), lambda b, h, c: (l, 0, h))]
    args = [qkvg, qkvg, qkvg, qkvg, dec, qd, kd, gl, gn.reshape(gn.shape[0], 1, -1)]
    aliases = _chain(states, in_specs, args, 1)
    nl, lb, lsel = _layer_block(states, l)
    return pl.pallas_call(
        functools.partial(_ret_kernel, lsel=lsel),
        grid=(B, H // P, T // L),
        in_specs=in_specs,
        out_specs=[pl.BlockSpec((1, L, P * RET_DV), lambda b, h, c: (b, c, h)),
                   pl.BlockSpec((nl, 1, P, RET_DK, RET_DV), lambda b, h, c: (lb, b, h, 0, 0))],
        out_shape=[jax.ShapeDtypeStruct((B, T, H * RET_DV), BF16),
                   jax.ShapeDtypeStruct((N_A, B, H, RET_DK, RET_DV), F32)],
        input_output_aliases=aliases,
        compiler_params=_params(("arbitrary", "arbitrary", "arbitrary")),
        name="retention_prompt",
    )(*args)


def _ret_step_kernel(x_ref, s_ref, dec_ref, qd_ref, kd_ref, gl_ref, gn_ref, *rest, seq, lsel):
    o_ref, so_ref = rest[-2:]
    H = RET_HEADS
    zpad = jnp.zeros((RET_PAD - seq, RET_DV), F32)
    for other in range(so_ref.shape[0]):
        if other != lsel:
            so_ref[other] = jnp.zeros(so_ref.shape[1:], F32)

    def padded(col0, width):
        return jnp.concatenate([x_ref[:, col0:col0 + width], zpad[:, :width]], axis=0)

    for h in range(H):
        q = padded(h * RET_DK, RET_DK)
        k = padded(H * RET_DK + h * RET_DK, RET_DK)
        v = padded(2 * H * RET_DK + h * RET_DV, RET_DV)
        gate = padded(2 * H * RET_DK + H * RET_DV + h * RET_DV, RET_DV)
        o, s_new = _retention_head(q, k, v, gate, s_ref[0, 0, h], dec_ref[h], qd_ref[h], kd_ref[h],
                                   gl_ref[h], gn_ref[0, :, h * RET_DV:(h + 1) * RET_DV])
        so_ref[lsel, 0, h] = s_new
        o_ref[:, h * RET_DV:(h + 1) * RET_DV] = o[:seq]


def retention_step(qkvg, state, gn, l, states, *, seq):
    rows, width = qkvg.shape
    nb = rows // seq
    H = RET_HEADS
    dec, qd, kd, gl = _decay_tables(seq, RET_PAD)
    whole = lambda shape: pl.BlockSpec(shape, lambda b: (0,) * len(shape))
    in_specs = [pl.BlockSpec((seq, width), lambda b: (b, 0)),
                pl.BlockSpec((1, 1, H, RET_DK, RET_DV), lambda b: (l, b, 0, 0, 0)),
                whole((H, RET_PAD, RET_PAD)), whole((H, RET_PAD, 1)), whole((H, RET_PAD, 1)), whole((H, 1, 1)),
                pl.BlockSpec((1, 1, H * RET_DV), lambda b: (l, 0, 0))]
    args = [qkvg, state, dec, qd, kd, gl, gn.reshape(gn.shape[0], 1, -1)]
    aliases = _chain(states, in_specs, args, 1)
    nl, lb, lsel = _layer_block(states, l)
    return pl.pallas_call(
        functools.partial(_ret_step_kernel, seq=seq, lsel=lsel),
        grid=(nb,),
        in_specs=in_specs,
        out_specs=[pl.BlockSpec((seq, H * RET_DV), lambda b: (b, 0)),
                   pl.BlockSpec((nl, 1, H, RET_DK, RET_DV), lambda b: (lb, b, 0, 0, 0))],
        out_shape=[jax.ShapeDtypeStruct((rows, H * RET_DV), F32),
                   jax.ShapeDtypeStruct(state.shape, F32)],
        input_output_aliases=aliases,
        compiler_params=_params(("arbitrary",)),
        name="retention_step",
    )(*args)


def _head_pairs(kcat, vcat):
    lo = lax.broadcasted_iota(jnp.int32, (kcat.shape[0], LANES), 1) < HEAD_DIM
    for pair in range(N_KV_HEADS // 2):
        kp = kcat[:, pair * LANES:(pair + 1) * LANES]
        vp = vcat[:, pair * LANES:(pair + 1) * LANES]
        kr = pltpu.roll(kp, HEAD_DIM, axis=1)
        vr = pltpu.roll(vp, HEAD_DIM, axis=1)
        for sub in range(2):
            ka, kb_ = (kp, kr) if sub == 0 else (kr, kp)
            va, vb_ = (vp, vr) if sub == 0 else (vr, vp)
            halves = [(jnp.where(lo, ka, 0.0).astype(BF16), jnp.where(lo, va, 0.0).astype(BF16)),
                      (jnp.where(lo, 0.0, kb_).astype(BF16), jnp.where(lo, 0.0, vb_).astype(BF16))]
            yield pair * 2 + sub, halves


def _sink_softmax(s, ok, sink):
    s = jnp.where(ok, s * (HEAD_DIM ** -0.5), NEG_INF)
    m = jnp.maximum(jnp.max(s, axis=-1, keepdims=True), sink)
    p = jnp.exp(s - m)
    return p * (1.0 / (jnp.sum(p, axis=-1, keepdims=True) + jnp.exp(sink - m)))


def _attend(q_of, kcat, vcat, ok, sink_ref, sink0, rows, store):
    pieces = GQA_GROUPS // 2
    for kh, halves in _head_pairs(kcat, vcat):
        base = kh * GQA_GROUPS * HEAD_DIM
        qs = jnp.concatenate([q_of(base + g * LANES) for g in range(pieces)], axis=0).astype(BF16)
        out = None
        for half, (kk, vv) in enumerate(halves):
            s = _bdot_nt(qs, kk)
            p = jnp.concatenate(
                [_sink_softmax(s[g * rows:(g + 1) * rows], ok, sink_ref[sink0 + kh * GQA_GROUPS + 2 * g + half])
                 for g in range(pieces)], axis=0)
            o = _bdot(p, vv)
            out = o if out is None else out + o
        for g in range(pieces):
            store(base + g * LANES, out[g * rows:(g + 1) * rows])


def _swa_kernel(sink_ref, q_ref, kp_ref, kc_ref, vp_ref, vc_ref, o_ref, *, sink0):
    i = pl.program_id(1)
    kcat = jnp.concatenate([kp_ref[0], kc_ref[0]], axis=0)
    vcat = jnp.concatenate([vp_ref[0], vc_ref[0]], axis=0)
    ql = lax.broadcasted_iota(jnp.int32, (ATT_BLOCK, 2 * ATT_BLOCK), 0)
    km = lax.broadcasted_iota(jnp.int32, (ATT_BLOCK, 2 * ATT_BLOCK), 1)
    no_prev = jnp.where(i > 0, 0, 4 * ATT_BLOCK)
    ok = ((km < ATT_BLOCK) & (km >= ql + no_prev)) | ((km >= ATT_BLOCK) & (km - ATT_BLOCK <= ql))

    def store(col0, val):
        o_ref[0, :, col0:col0 + LANES] = val.astype(o_ref.dtype)

    _attend(lambda c0: q_ref[0, :, c0:c0 + LANES], kcat, vcat, ok, sink_ref, sink0, ATT_BLOCK, store)


def swa_prompt(q, kv, sinks, j):
    B, T, DQ = q.shape
    DKV = kv.shape[2] // 2
    cur = lambda b, i: (b, i, 0)
    return pl.pallas_call(
        functools.partial(_swa_kernel, sink0=j * N_Q_HEADS),
        grid=(B, T // ATT_BLOCK),
        in_specs=[pl.BlockSpec(memory_space=pltpu.SMEM),
                  pl.BlockSpec((1, ATT_BLOCK, DQ), cur),
                  pl.BlockSpec((1, ATT_BLOCK, DKV), lambda b, i: (b, jnp.maximum(i - 1, 0), 0)),
                  pl.BlockSpec((1, ATT_BLOCK, DKV), lambda b, i: (b, i, 0)),
                  pl.BlockSpec((1, ATT_BLOCK, DKV), lambda b, i: (b, jnp.maximum(i - 1, 0), 1)),
                  pl.BlockSpec((1, ATT_BLOCK, DKV), lambda b, i: (b, i, 1))],
        out_specs=pl.BlockSpec((1, ATT_BLOCK, DQ), cur),
        out_shape=jax.ShapeDtypeStruct((B, T, DQ), BF16),
        compiler_params=_params(("arbitrary", "arbitrary")),
        name="swa_prompt",
    )(sinks, q, kv, kv, kv, kv)


def _swa_step_kernel(sink_ref, q_ref, ck_ref, nk_ref, cv_ref, nv_ref, o_ref, *, seq, wb, sink0):
    nk = 2 * ATT_BLOCK
    zpad = jnp.zeros((nk - wb - seq, nk_ref.shape[1]), F32)
    t = lax.broadcasted_iota(jnp.int32, (seq, nk), 0)
    s = lax.broadcasted_iota(jnp.int32, (seq, nk), 1)
    rel = t + wb - s
    ok = (rel >= 0) & (rel <= WINDOW)
    for e in range(ck_ref.shape[0]):
        rows = slice(e * seq, (e + 1) * seq)
        kcat = jnp.concatenate([ck_ref[e], nk_ref[rows, :], zpad], axis=0)
        vcat = jnp.concatenate([cv_ref[e], nv_ref[rows, :], zpad], axis=0)

        def store(col0, val, rows=rows):
            o_ref[rows, col0:col0 + LANES] = val

        _attend(lambda c0, rows=rows: q_ref[rows, c0:c0 + LANES], kcat, vcat, ok, sink_ref, sink0, seq, store)


def swa_step(q, cache_k, cache_v, kv_new, sinks, j, *, seq, per_step=4):
    rows, DQ = q.shape
    nb, wb, DKV = cache_k.shape
    E = per_step
    return pl.pallas_call(
        functools.partial(_swa_step_kernel, seq=seq, wb=wb, sink0=j * N_Q_HEADS),
        grid=(nb // E,),
        in_specs=[pl.BlockSpec(memory_space=pltpu.SMEM),
                  pl.BlockSpec((E * seq, DQ), lambda b: (b, 0)),
                  pl.BlockSpec((E, wb, DKV), lambda b: (b, 0, 0)), pl.BlockSpec((E * seq, DKV), lambda b: (b, 0)),
                  pl.BlockSpec((E, wb, DKV), lambda b: (b, 0, 0)), pl.BlockSpec((E * seq, DKV), lambda b: (b, 1))],
        out_specs=pl.BlockSpec((E * seq, DQ), lambda b: (b, 0)),
        out_shape=jax.ShapeDtypeStruct((rows, DQ), F32),
        compiler_params=_params(("arbitrary",)),
        name="swa_step",
    )(sinks, q, cache_k, kv_new, cache_v, kv_new)


def _rope_tables_partial(pos):
    half = ROPE_DIM // 2
    inv = ROPE_THETA ** (-jnp.arange(half, dtype=F32) * 2.0 / ROPE_DIM)
    ang = pos.astype(F32)[:, None] * inv[None, :]
    cos, sin = jnp.cos(ang), jnp.sin(ang)
    T = pos.shape[0]
    ones = jnp.ones((T, HEAD_DIM - ROPE_DIM), F32)
    zeros = jnp.zeros((T, HEAD_DIM - ROPE_DIM), F32)
    zh = jnp.zeros((T, half), F32)
    c = jnp.concatenate([cos, cos, ones], axis=1)
    s1 = jnp.concatenate([-sin, zh, zeros], axis=1)
    s2 = jnp.concatenate([zh, sin, zeros], axis=1)
    rep = LANES // HEAD_DIM
    return tuple(jnp.tile(a, (1, rep)) for a in (c, s1, s2))


def _final_norm_kernel(x_ref, g_ref, o_ref):
    x = x_ref[0]
    ms = jnp.mean(x * x, axis=-1, keepdims=True)
    o_ref[0] = x * lax.rsqrt(ms + EPS) * g_ref[...]


def final_norm(x, g, tm):
    B, T, D = x.shape
    return pl.pallas_call(
        _final_norm_kernel,
        grid=(B, T // tm),
        in_specs=[pl.BlockSpec((1, tm, D), lambda b, i: (b, i, 0)), pl.BlockSpec((1, D), lambda b, i: (0, 0))],
        out_specs=pl.BlockSpec((1, tm, D), lambda b, i: (b, i, 0)),
        out_shape=jax.ShapeDtypeStruct((B, T, D), F32),
        compiler_params=_params(("arbitrary", "arbitrary")),
        name="final_norm",
    )(x, g.reshape(1, D))


def kernel(x_prompt, x_sample, state_ret, cache_win_k, cache_win_v, state_conv, c_prompt, c_sample,
           w_ada, b_ada, norm_mix, norm_ffn, ret_w_in, ret_gn, ret_w_out,
           kv_norm, kv_w_ada, kv_b_ada, w_kv, att_w_q, att_sinks, att_w_o,
           ffn_w_up, ffn_conv_w, ffn_conv_b, ffn_w_down, norm_f):
    D = D_MODEL
    BP, TP, _ = x_prompt.shape
    BS, TS, _ = x_sample.shape
    RS = BS * TS
    KV = N_KV_HEADS * HEAD_DIM
    QK = RET_HEADS * RET_DK

    c_all = jnp.concatenate([c_sample, c_prompt], axis=0)
    c_all = jnp.pad(c_all, ((0, -c_all.shape[0] % PACK), (0, 0)))
    mods = Mods(ada_mods(c_all, w_ada, b_ada), BS)
    kv_mods = Mods(ada_mods(c_all, kv_w_ada[None], kv_b_ada[None]), BS)
    tile_col = lambda b, i, j: j

    pos_p = jnp.arange(TP, dtype=jnp.int32)
    pos_s = PAST_LEN + jnp.arange(TS, dtype=jnp.int32)
    tabs_p = _rope_tables_partial(pos_p)
    tabs_s = tuple(jnp.tile(a, (BS, 1)) for a in _rope_tables_partial(pos_s))
    full_p = _rope_tables_full(pos_p)
    full_s = tuple(jnp.tile(a, (BS, 1)) for a in _rope_tables_full(pos_s))
    sinks = att_sinks.reshape(-1)
    kv_norm1, w_kv1 = kv_norm[None], w_kv[None]

    x = x_sample.reshape(1, RS, D)
    wb = cache_win_k.shape[1]
    ck = cache_win_k.reshape(BS, wb, KV)
    cv = cache_win_v.reshape(BS, wb, KV)
    ret_s = None
    conv_s = []
    w16 = [dict() for _ in range(DEPTH)]
    for l in range(DEPTH):
        if l == N_A:
            kv_s, w16_kv = proj(x, kv_norm1, 0, kv_mods.sample(0, 0, D), kv_mods.sample(0, 1, D), w_kv1, 0, tm=RS,
                                tn=2 * KV, tok=TS, out_dtype=F32, rope="partial", rope_tabs=tabs_s, rope_cols=KV,
                                emit_w=True)
            kv_s = kv_s[0]
        sh1, sc1, sh2, sc2 = (mods.sample(l, k, D) for k in (0, 1, 3, 4))
        if l < N_A:
            qkvg, w16[l]["in"] = proj(x, norm_mix, l, sh1, sc1, ret_w_in, l, tm=RS, tn=SAMPLE_TN_PROJ, tok=TS,
                                      out_dtype=F32, rope="full", rope_tabs=full_s, rope_cols=2 * QK, scale_from=QK,
                                      emit_w=True)
            a, ret_s = retention_step(qkvg[0], state_ret, ret_gn, l, ret_s, seq=TS)
            w_mix, wl = ret_w_out, l
        else:
            j = l - N_A
            q, w16[l]["in"] = proj(x, norm_mix, l, sh1, sc1, att_w_q, j, tm=RS, tn=SAMPLE_TN_PROJ, tok=TS,
                                   out_dtype=F32, rope="partial", rope_tabs=tabs_s, rope_cols=D, emit_w=True)
            a = swa_step(q[0], ck, cv, kv_s, sinks, j, seq=TS)
            w_mix, wl = att_w_o, j
        tn = SAMPLE_TN_OUT[w_mix.shape[1]]
        x, w16[l]["out"] = out_proj(a[None], w_mix, wl, x, mods.sample(l, 2, tn, tile_col), tm=RS, tn=tn, tok=TS,
                                    emit_w=True)
        z, ua, ub, w16[l]["up_a"], w16[l]["up_b"] = upconv_step(x, norm_ffn, l, sh2, sc2, ffn_w_up, ffn_conv_w,
                                                                ffn_conv_b, state_conv, seq=TS, tn=SAMPLE_TN_UP)
        conv_s.append(jnp.concatenate([ua.reshape(BS, TS, D_FF)[:, TS - (CONV_W - 1):],
                                       ub.reshape(BS, TS, D_FF)[:, TS - (CONV_W - 1):]], axis=-1))
        tn = SAMPLE_TN_OUT[D_FF]
        x, w16[l]["down"] = out_proj(z[None], ffn_w_down, l, x, mods.sample(l, 5, tn, tile_col), tm=RS, tn=tn, tok=TS,
                                     emit_w=True)
    y_sample = final_norm(x, norm_f, RS).reshape(BS, TS, D)

    x = x_prompt
    ret_p = None
    conv_p = []
    for l in range(DEPTH):
        if l == N_A:
            kv_p = proj_prompt(x, kv_norm1, 0, kv_mods, (0, 0, 1), w16_kv, tm=PROMPT_TM, tn=2 * KV, out_dtype=F32,
                               rope="partial", rope_tabs=tabs_p, rope_cols=KV)
        if l < N_A:
            qkvg = proj_prompt(x, norm_mix, l, mods, (l, 0, 1), w16[l]["in"], tm=PROMPT_TM, tn=PROMPT_TN_PROJ,
                               out_dtype=F32, rope="full", rope_tabs=full_p, rope_cols=2 * QK, scale_from=QK)
            a, ret_p = retention_prompt(qkvg, ret_gn, l, ret_p)
        else:
            q = proj_prompt(x, norm_mix, l, mods, (l, 0, 1), w16[l]["in"], tm=PROMPT_TM, tn=PROMPT_TN_PROJ,
                            out_dtype=BF16, rope="partial", rope_tabs=tabs_p, rope_cols=D)
            a = swa_prompt(q, kv_p, sinks, l - N_A)
        x = out_proj(a, w16[l]["out"], 0, x, mods.prompt(l, 2, PROMPT_TN_OUT, tile_col), tm=PROMPT_TM,
                     tn=PROMPT_TN_OUT, tok=None)
        z, st = upconv_prompt(x, norm_ffn, l, mods, (l, 3, 4), w16[l]["up_a"], w16[l]["up_b"], ffn_conv_w, ffn_conv_b,
                              tm=PROMPT_TM, tn=PROMPT_TN_UP)
        conv_p.append(st[:, -1, :, SUBLANES - (CONV_W - 1):, :].transpose(0, 2, 1, 3).reshape(BP, CONV_W - 1, 2 * D_FF))
        if l < DEPTH - 1:
            x = out_proj(z, w16[l]["down"], 0, x, mods.prompt(l, 5, PROMPT_TN_OUT, tile_col), tm=PROMPT_TM,
                         tn=PROMPT_TN_OUT, tok=None)
        else:
            y_prompt = out_proj_final(z, w16[l]["down"], x, mods.prompt(l, 5, D), norm_f, tm=PROMPT_TM_FINAL)
    wp = min(WINDOW, TP)
    win_k_prompt = kv_p[:, -wp:, :KV].reshape(BP, wp, N_KV_HEADS, HEAD_DIM)
    win_v_prompt = kv_p[:, -wp:, KV:].reshape(BP, wp, N_KV_HEADS, HEAD_DIM)
    k_all = jnp.concatenate([ck, kv_s[:, :KV].reshape(BS, TS, KV)], axis=1)
    v_all = jnp.concatenate([cv, kv_s[:, KV:].reshape(BS, TS, KV)], axis=1)
    win_k_sample = k_all[:, -wb:].reshape(BS, wb, N_KV_HEADS, HEAD_DIM)
    win_v_sample = v_all[:, -wb:].reshape(BS, wb, N_KV_HEADS, HEAD_DIM)

    return (y_prompt, y_sample, ret_p, ret_s, win_k_prompt, win_v_prompt,
            win_k_sample, win_v_sample, jnp.stack(conv_p), jnp.stack(conv_s))
```

```python
import functools

import jax
import jax.numpy as jnp
from jax import lax
from jax.experimental import pallas as pl
from jax.experimental.pallas import tpu as pltpu

F32 = jnp.float32
BF16 = jnp.bfloat16

D_MODEL = 2048
DEPTH = 4
PAST_LEN = 16384
N_A = DEPTH // 2
RET_HEADS = 8
RET_DK = D_MODEL // RET_HEADS
RET_DV = 2 * D_MODEL // RET_HEADS
RET_ROPE_BASE = 10000.0
N_Q_HEADS = 32
N_KV_HEADS = 4
HEAD_DIM = D_MODEL // N_Q_HEADS
GQA_GROUPS = N_Q_HEADS // N_KV_HEADS
ROPE_DIM = HEAD_DIM // 4
ROPE_THETA = 500000.0
WINDOW = 128
ATT_BLOCK = 128
D_FF = 2 * D_MODEL
CONV_W = 3
N_MOD = 6
EPS = 1e-6
NEG_INF = -1e30

LANES = 128
SUBLANES = 8
PACK = 16
HALO = PACK
RET_BLOCK = 512
RET_HEADS_PER_STEP = 4
RET_PAD = PACK
VMEM_LIMIT = 56 * 1024 * 1024
PROMPT_TM = 1024
PROMPT_TN_PROJ = 1024
PROMPT_TN_OUT = 1024
PROMPT_TN_UP = 512
PROMPT_TM_FINAL = 512
SAMPLE_TN_PROJ = 1024
SAMPLE_TN_UP = 512
SAMPLE_TN_OUT = {RET_HEADS * RET_DV: 512, D_MODEL: 1024}


def _params(sem):
    return pltpu.CompilerParams(dimension_semantics=sem, vmem_limit_bytes=VMEM_LIMIT)


def _silu(x):
    return x * jax.nn.sigmoid(x)


def _bdot(a, b):
    return jnp.dot(a.astype(BF16), b.astype(BF16), preferred_element_type=F32)


def _bdot_nt(a, b):
    return lax.dot_general(a.astype(BF16), b.astype(BF16), (((1,), (1,)), ((), ())), preferred_element_type=F32)


class Mods:
    def __init__(self, arr, n_sample):
        self.a3 = arr
        self.a4 = arr.reshape(arr.shape[0], arr.shape[1], 1, arr.shape[2])
        self.n_sample = n_sample

    def prompt(self, l, k, width, col=lambda *g: 0, batch=lambda *g: g[0]):
        per = D_MODEL // width
        off = self.n_sample
        return self.a4, pl.BlockSpec((1, 1, 1, width), lambda *g: (l, off + batch(*g), 0, k * per + col(*g)))

    def sample(self, l, k, width, col=lambda *g: 0):
        per = D_MODEL // width
        return self.a3, pl.BlockSpec((1, self.n_sample, width), lambda *g: (l, 0, k * per + col(*g)))


def _ada_kernel(c_ref, w_ref, b_ref, o_ref):
    o_ref[0] = _bdot(_silu(c_ref[...]), w_ref[0]) + b_ref[0]


def ada_mods(c_all, w, b, tn=1024):
    L, D, N = w.shape
    R = c_all.shape[0]
    return pl.pallas_call(
        _ada_kernel,
        grid=(L, N // tn),
        in_specs=[pl.BlockSpec((R, D), lambda l, j: (0, 0)),
                  pl.BlockSpec((1, D, tn), lambda l, j: (l, 0, j)),
                  pl.BlockSpec((1, 1, tn), lambda l, j: (l, 0, j))],
        out_specs=pl.BlockSpec((1, R, tn), lambda l, j: (l, 0, j)),
        out_shape=jax.ShapeDtypeStruct((L, R, N), F32),
        compiler_params=_params(("arbitrary", "arbitrary")),
        name="ada_mods",
    )(c_all, w, b.reshape(L, 1, N))


def _store_normed(h_scr, r_scr, row0, x_ref, g_ref, sc_ref, sh_ref, rows, tok):
    def stats(r, carry):
        r0 = pl.multiple_of(r * PACK, PACK)
        x = x_ref[0, pl.ds(r0, PACK), :]
        r_scr[pl.ds(r0, PACK), :] = lax.rsqrt(jnp.mean(x * x, axis=-1, keepdims=True) + EPS)
        return carry

    lax.fori_loop(0, rows // PACK, stats, 0, unroll=4)
    g = g_ref[0]

    def piece(start, n, sc, sh):
        return x_ref[0, pl.ds(start, n), :] * r_scr[pl.ds(start, n), :] * g * (1.0 + sc) + sh

    def apply(r, carry):
        r0 = pl.multiple_of(r * PACK, PACK)
        if tok is None:
            h = piece(r0, PACK, sc_ref[0, 0], sh_ref[0, 0])
        else:
            parts = []
            for s in range(PACK // tok):
                m = r * (PACK // tok) + s
                start = pl.multiple_of(r0 + s * tok, tok)
                parts.append(piece(start, tok, sc_ref[0, pl.ds(m, 1), :], sh_ref[0, pl.ds(m, 1), :]))
            h = jnp.concatenate(parts, axis=0)
        h_scr[pl.ds(pl.multiple_of(row0 + r0, PACK), PACK), :] = h.astype(BF16)
        return carry

    lax.fori_loop(0, rows // PACK, apply, 0, unroll=2)


def _rope64(a, c, s1, s2):
    return a * c + pltpu.roll(a, LANES - ROPE_DIM // 2, axis=1) * s1 + pltpu.roll(a, ROPE_DIM // 2, axis=1) * s2


def _proj_kernel(*refs, tm, tn, tok, rope, rope_cols, scale_from, emit_w):
    x_ref, g_ref, sh_ref, sc_ref, w_ref = refs[:5]
    if rope:
        t1_ref, t2_ref, t3_ref = refs[5:8]
    o_ref = refs[-4] if emit_w else refs[-3]
    h_scr, r_scr = refs[-2:]
    j = pl.program_id(2)

    @pl.when(j == 0)
    def _():
        _store_normed(h_scr, r_scr, 0, x_ref, g_ref, sc_ref, sh_ref, tm, tok)

    wb = w_ref[0].astype(BF16)
    if emit_w:
        refs[-3][0] = wb
    acc = jnp.dot(h_scr[...], wb, preferred_element_type=F32)
    _proj_epilogue(acc, o_ref, j, tn, rope, rope_cols, scale_from, refs[5:8])


def _proj_epilogue(acc, o_ref, j, tn, rope, rope_cols, scale_from, tabs):
    if rope is None:
        o_ref[0] = acc.astype(o_ref.dtype)
    elif rope == "partial":
        t1_ref, t2_ref, t3_ref = tabs
        for c in range(tn // LANES):
            a = acc[:, c * LANES:(c + 1) * LANES]
            if c * LANES < rope_cols:
                a = _rope64(a, t1_ref[...], t2_ref[...], t3_ref[...])
            o_ref[0, :, c * LANES:(c + 1) * LANES] = a.astype(o_ref.dtype)
    else:
        t1_ref, t2_ref, _ = tabs
        rotated = j * tn < rope_cols
        cos = jnp.where(rotated, t1_ref[...], 1.0)
        sin = jnp.where(rotated, t2_ref[...], 0.0)
        scale = jnp.where(rotated & (j * tn >= scale_from), RET_DK ** -0.5, 1.0).astype(F32)
        for hd in range(tn // RET_DK):
            x1 = acc[:, hd * RET_DK:hd * RET_DK + LANES]
            x2 = acc[:, hd * RET_DK + LANES:(hd + 1) * RET_DK]
            o_ref[0, :, hd * RET_DK:hd * RET_DK + LANES] = ((x1 * cos - x2 * sin) * scale).astype(o_ref.dtype)
            o_ref[0, :, hd * RET_DK + LANES:(hd + 1) * RET_DK] = ((x2 * cos + x1 * sin) * scale).astype(o_ref.dtype)


def _norm_rows(h_scr, slot, row0, x_ref, g_ref, sc_ref, sh_ref, start, n):
    g = g_ref[0]
    sc1 = 1.0 + sc_ref[0, 0]
    sh = sh_ref[0, 0]
    for c in range(n // PACK):
        r0 = pl.multiple_of(start + c * PACK, PACK)
        x = x_ref[0, pl.ds(r0, PACK), :]
        r = lax.rsqrt(jnp.mean(x * x, axis=-1, keepdims=True) + EPS)
        h_scr[slot, pl.ds(pl.multiple_of(row0 + r0, PACK), PACK), :] = (x * r * g * sc1 + sh).astype(BF16)


def _lag_rows(tm, nj):
    return PACK * -(-tm // (PACK * nj))


def _proj_lag_kernel(*refs, tm, tn, nj, rope, rope_cols, scale_from):
    x_ref, g_ref, sh_ref, sc_ref, w_ref = refs[:5]
    o_ref, h_scr = refs[-2:]
    t = pl.program_id(0)
    j = pl.program_id(1)
    n = _lag_rows(tm, nj)
    start = jnp.minimum(j * n, tm - n)
    wslot = t % 2

    @pl.when(t == 0)
    def _():
        _norm_rows(h_scr, wslot, 0, x_ref, g_ref, sc_ref, sh_ref, start, n)

    @pl.when(t > 0)
    def _():
        _norm_rows(h_scr, wslot, 0, x_ref, g_ref, sc_ref, sh_ref, start, n)
        acc = jnp.dot(h_scr[1 - wslot], w_ref[0], preferred_element_type=F32)
        _proj_epilogue(acc, o_ref, j, tn, rope, rope_cols, scale_from, refs[5:8])


def proj_prompt(x, g, l, mods, ks, w, *, tm, tn, out_dtype, rope=None, rope_tabs=None, rope_cols=0, scale_from=0):
    B, T, D = x.shape
    N = w.shape[2]
    assert rope != "partial" or rope_cols == N or tn == N
    tpb = T // tm
    nt = B * tpb
    nj = N // tn
    cur = lambda t: jnp.minimum(t, nt - 1)
    prev = lambda t: jnp.maximum(t - 1, 0)
    colj = lambda t, j: jnp.where(t > 0, j, 0)
    ml, ksh, ksc = ks
    sh = mods.prompt(ml, ksh, D, batch=lambda t, j: cur(t) // tpb)
    sc = mods.prompt(ml, ksc, D, batch=lambda t, j: cur(t) // tpb)
    in_specs = [pl.BlockSpec((1, tm, D), lambda t, j: (cur(t) // tpb, cur(t) % tpb, 0)),
                pl.BlockSpec((1, 1, D), lambda t, j: (l, 0, 0)),
                sh[1], sc[1],
                pl.BlockSpec((1, D, tn), lambda t, j: (0, 0, colj(t, j)))]
    args = [x, g.reshape(g.shape[0], 1, D), sh[0], sc[0], w]
    if rope:
        tabs = list(rope_tabs) + ([rope_tabs[0]] if len(rope_tabs) == 2 else [])
        in_specs += [pl.BlockSpec((tm, LANES), lambda t, j: (prev(t) % tpb, 0))] * 3
        args += tabs
    return pl.pallas_call(
        functools.partial(_proj_lag_kernel, tm=tm, tn=tn, nj=nj, rope=rope, rope_cols=rope_cols, scale_from=scale_from),
        grid=(nt + 1, nj),
        in_specs=in_specs,
        out_specs=pl.BlockSpec((1, tm, tn), lambda t, j: (prev(t) // tpb, prev(t) % tpb, colj(t, j))),
        out_shape=jax.ShapeDtypeStruct((B, T, N), out_dtype),
        scratch_shapes=[pltpu.VMEM((2, tm, D), BF16)],
        compiler_params=_params(("arbitrary", "arbitrary")),
        name="proj_prompt",
    )(*args)


def proj(x, g, l, sh, sc, w, wl, *, tm, tn, tok, out_dtype, rope=None, rope_tabs=None, rope_cols=0, scale_from=0,
         emit_w=False):
    B, T, D = x.shape
    N = w.shape[2]
    assert rope != "partial" or rope_cols == N or tn == N
    assert not emit_w or (B == 1 and T == tm)
    out_specs = [pl.BlockSpec((1, tm, tn), lambda b, i, j: (b, i, j))]
    out_shape = [jax.ShapeDtypeStruct((B, T, N), out_dtype)]
    if emit_w:
        out_specs.append(pl.BlockSpec((1, D, tn), lambda b, i, j: (0, 0, j)))
        out_shape.append(jax.ShapeDtypeStruct((1, D, N), BF16))
    in_specs = [pl.BlockSpec((1, tm, D), lambda b, i, j: (b, i, 0)),
                pl.BlockSpec((1, 1, D), lambda b, i, j: (l, 0, 0)),
                sh[1], sc[1],
                pl.BlockSpec((1, D, tn), lambda b, i, j: (wl, 0, j))]
    args = [x, g.reshape(g.shape[0], 1, D), sh[0], sc[0], w]
    if rope:
        tabs = list(rope_tabs) + ([rope_tabs[0]] if len(rope_tabs) == 2 else [])
        in_specs += [pl.BlockSpec((tm, LANES), lambda b, i, j: (i, 0))] * 3
        args += tabs
    outs = pl.pallas_call(
        functools.partial(_proj_kernel, tm=tm, tn=tn, tok=tok, rope=rope, rope_cols=rope_cols, scale_from=scale_from,
                          emit_w=emit_w),
        grid=(B, T // tm, N // tn),
        in_specs=in_specs,
        out_specs=out_specs,
        out_shape=out_shape,
        scratch_shapes=[pltpu.VMEM((tm, D), BF16), pltpu.VMEM((tm, 1), F32)],
        compiler_params=_params(("arbitrary", "arbitrary", "arbitrary")),
        name="proj",
    )(*args)
    return outs if emit_w else outs[0]


def _conv_gate(ua, ub, cwa_ref, cwb_ref, cba_ref, cbb_ref):
    def conv(u, cw_ref, cb_ref):
        acc = cb_ref[0] + cw_ref[0, 0:1, :] * u[0]
        acc = acc + cw_ref[0, 1:2, :] * u[1]
        return acc + cw_ref[0, 2:3, :] * u[2]

    return _silu(conv(ua, cwa_ref, cba_ref)) * conv(ub, cwb_ref, cbb_ref)


def _upconv_kernel(x_ref, g_ref, sh_ref, sc_ref, wa_ref, wb_ref, cwa_ref, cwb_ref, cba_ref, cbb_ref,
                   z_ref, st_ref, h_scr, *, tm, nj, tpb, nt):
    t = pl.program_id(0)
    n = _lag_rows(tm, nj)
    start = jnp.minimum(pl.program_id(1) * n, tm - n)
    wslot = t % 2
    zero_halo = jnp.zeros((HALO, h_scr.shape[2]), BF16)

    @pl.when(t == 0)
    def _():
        _norm_rows(h_scr, wslot, HALO, x_ref, g_ref, sc_ref, sh_ref, start, n)
        h_scr[wslot, 0:HALO, :] = zero_halo

    @pl.when(t > 0)
    def _():
        _norm_rows(h_scr, wslot, HALO, x_ref, g_ref, sc_ref, sh_ref, start, n)
        tail = h_scr[1 - wslot, tm:tm + HALO, :]
        h_scr[wslot, 0:HALO, :] = jnp.where(jnp.minimum(t, nt - 1) % tpb > 0, tail, zero_halo)
        h = h_scr[1 - wslot]

        def taps(w_ref):
            u = jnp.dot(h, w_ref[0], preferred_element_type=F32)
            return (pltpu.roll(u, 2, axis=0)[HALO:], pltpu.roll(u, 1, axis=0)[HALO:], u[HALO:])

        ua = taps(wa_ref)
        ub = taps(wb_ref)
        z_ref[0] = _conv_gate(ua, ub, cwa_ref, cwb_ref, cba_ref, cbb_ref).astype(z_ref.dtype)
        st_ref[0, 0, 0] = ua[2][tm - SUBLANES:, :]
        st_ref[0, 0, 1] = ub[2][tm - SUBLANES:, :]


def _halves_specs(shape, l, nj, col):
    return [pl.BlockSpec(shape, lambda *g: (l, 0, col(*g))), pl.BlockSpec(shape, lambda *g: (l, 0, col(*g) + nj))]


def upconv_prompt(x, g, l, mods, ks, wa, wb, conv_w, conv_b, *, tm=1024, tn=512):
    B, T, D = x.shape
    F = wa.shape[2]
    nj = F // tn
    L = conv_w.shape[0]
    tpb = T // tm
    nt = B * tpb
    cur = lambda t: jnp.minimum(t, nt - 1)
    prev = lambda t: jnp.maximum(t - 1, 0)
    colj = lambda t, j: jnp.where(t > 0, j, 0)
    ml, ksh, ksc = ks
    sh = mods.prompt(ml, ksh, D, batch=lambda t, j: cur(t) // tpb)
    sc = mods.prompt(ml, ksc, D, batch=lambda t, j: cur(t) // tpb)
    in_specs = [pl.BlockSpec((1, tm, D), lambda t, j: (cur(t) // tpb, cur(t) % tpb, 0)),
                pl.BlockSpec((1, 1, D), lambda t, j: (l, 0, 0)),
                sh[1], sc[1],
                pl.BlockSpec((1, D, tn), lambda t, j: (0, 0, colj(t, j))),
                pl.BlockSpec((1, D, tn), lambda t, j: (0, 0, colj(t, j)))]
    in_specs += _halves_specs((1, CONV_W, tn), l, nj, colj) + _halves_specs((1, 1, tn), l, nj, colj)
    cb = conv_b.reshape(L, 1, 2 * F)
    return pl.pallas_call(
        functools.partial(_upconv_kernel, tm=tm, nj=nj, tpb=tpb, nt=nt),
        grid=(nt + 1, nj),
        in_specs=in_specs,
        out_specs=[pl.BlockSpec((1, tm, tn), lambda t, j: (prev(t) // tpb, prev(t) % tpb, colj(t, j))),
                   pl.BlockSpec((1, 1, 2, SUBLANES, tn),
                                lambda t, j: (prev(t) // tpb, prev(t) % tpb, 0, 0, colj(t, j)))],
        out_shape=[jax.ShapeDtypeStruct((B, T, F), BF16),
                   jax.ShapeDtypeStruct((B, tpb, 2, SUBLANES, F), F32)],
        scratch_shapes=[pltpu.VMEM((2, HALO + tm, D), BF16)],
        compiler_params=_params(("arbitrary", "arbitrary")),
        name="upconv_prompt",
    )(x, g.reshape(L, 1, D), sh[0], sc[0], wa, wb, conv_w, conv_w, cb, cb)


def _upconv_step_kernel(x_ref, g_ref, sh_ref, sc_ref, wa_ref, wb_ref, cwa_ref, cwb_ref, cba_ref, cbb_ref,
                        sta_ref, stb_ref, z_ref, ua_ref, ub_ref, wao_ref, wbo_ref, h_scr, r_scr, *, rows, seq):
    @pl.when(pl.program_id(0) == 0)
    def _():
        _store_normed(h_scr, r_scr, 0, x_ref, g_ref, sc_ref, sh_ref, rows, seq)

    h = h_scr[...]
    nb = rows // seq

    def taps(w_ref, st_ref, u_ref, wo_ref):
        w = w_ref[0].astype(BF16)
        wo_ref[0] = w
        u = jnp.dot(h, w, preferred_element_type=F32)
        u_ref[...] = u
        u3 = u.reshape(nb, seq, u.shape[1])
        t = lax.broadcasted_iota(jnp.int32, u3.shape, 1)
        st0 = st_ref[0, :, 0:1, :]
        st1 = st_ref[0, :, 1:2, :]
        um1 = jnp.where(t == 0, st1, pltpu.roll(u3, 1, axis=1))
        um2 = jnp.where(t == 0, st0, jnp.where(t == 1, st1, pltpu.roll(u3, 2, axis=1)))
        return (um2, um1, u3)

    ua = taps(wa_ref, sta_ref, ua_ref, wao_ref)
    ub = taps(wb_ref, stb_ref, ub_ref, wbo_ref)
    z = _conv_gate(ua, ub, cwa_ref, cwb_ref, cba_ref, cbb_ref)
    z_ref[...] = z.reshape(rows, z.shape[2]).astype(z_ref.dtype)


def upconv_step(x, g, l, sh, sc, w_up, conv_w, conv_b, state, *, seq, tn=512):
    _, rows, D = x.shape
    F = w_up.shape[2] // 2
    nj = F // tn
    nb = rows // seq
    L = w_up.shape[0]
    col = lambda j: j
    in_specs = [pl.BlockSpec((1, rows, D), lambda j: (0, 0, 0)),
                pl.BlockSpec((1, 1, D), lambda j: (l, 0, 0)),
                sh[1], sc[1]]
    in_specs += _halves_specs((1, D, tn), l, nj, col) + _halves_specs((1, CONV_W, tn), l, nj, col)
    in_specs += _halves_specs((1, 1, tn), l, nj, col)
    in_specs += [pl.BlockSpec((1, nb, CONV_W - 1, tn), lambda j: (l, 0, 0, j)),
                 pl.BlockSpec((1, nb, CONV_W - 1, tn), lambda j: (l, 0, 0, j + nj))]
    cb = conv_b.reshape(L, 1, 2 * F)
    return pl.pallas_call(
        functools.partial(_upconv_step_kernel, rows=rows, seq=seq),
        grid=(nj,),
        in_specs=in_specs,
        out_specs=[pl.BlockSpec((rows, tn), lambda j: (0, j))] * 3 + [pl.BlockSpec((1, D, tn), lambda j: (0, 0, j))] * 2,
        out_shape=[jax.ShapeDtypeStruct((rows, F), BF16), jax.ShapeDtypeStruct((rows, F), F32),
                   jax.ShapeDtypeStruct((rows, F), F32), jax.ShapeDtypeStruct((1, D, F), BF16),
                   jax.ShapeDtypeStruct((1, D, F), BF16)],
        scratch_shapes=[pltpu.VMEM((rows, D), BF16), pltpu.VMEM((rows, 1), F32)],
        compiler_params=_params(("arbitrary",)),
        name="upconv_step",
    )(x, g.reshape(L, 1, D), sh[0], sc[0], w_up, w_up, conv_w, conv_w, cb, cb, state, state)


def _out_kernel(a_ref, w_ref, x_ref, gt_ref, o_ref, *wo_ref, tok):
    w = w_ref[0].astype(BF16)
    if wo_ref:
        wo_ref[0][0] = w
    y = jnp.dot(a_ref[0].astype(BF16), w, preferred_element_type=F32)
    if tok is None:
        o_ref[0] = x_ref[0] + gt_ref[0, 0] * y
    else:
        for b in range(y.shape[0] // tok):
            rows = slice(b * tok, (b + 1) * tok)
            o_ref[0, rows, :] = x_ref[0, rows, :] + gt_ref[0, b:b + 1, :] * y[rows]


def out_proj(a, w, l, x, gate, *, tm, tn, tok, emit_w=False):
    B, T, K = a.shape
    N = w.shape[2]
    assert not emit_w or (B == 1 and T == tm)
    out_specs = [pl.BlockSpec((1, tm, tn), lambda b, i, j: (b, i, j))]
    out_shape = [jax.ShapeDtypeStruct((B, T, N), F32)]
    if emit_w:
        out_specs.append(pl.BlockSpec((1, K, tn), lambda b, i, j: (0, 0, j)))
        out_shape.append(jax.ShapeDtypeStruct((1, K, N), BF16))
    outs = pl.pallas_call(
        functools.partial(_out_kernel, tok=tok),
        grid=(B, T // tm, N // tn),
        in_specs=[pl.BlockSpec((1, tm, K), lambda b, i, j: (b, i, 0)),
                  pl.BlockSpec((1, K, tn), lambda b, i, j: (l, 0, j)),
                  pl.BlockSpec((1, tm, tn), lambda b, i, j: (b, i, j)),
                  gate[1]],
        out_specs=out_specs,
        out_shape=out_shape,
        compiler_params=_params(("arbitrary", "arbitrary", "arbitrary")),
        name="out_proj",
    )(a, w, x, gate[0])
    return outs if emit_w else outs[0]


def _out_final_kernel(a_ref, w_ref, x_ref, gt_ref, nf_ref, o_ref):
    y = jnp.dot(a_ref[0], w_ref[0], preferred_element_type=F32)
    x = x_ref[0] + gt_ref[0, 0] * y
    ms = jnp.mean(x * x, axis=-1, keepdims=True)
    o_ref[0] = x * lax.rsqrt(ms + EPS) * nf_ref[...]


def out_proj_final(a, w, x, gate, norm_g, *, tm):
    B, T, K = a.shape
    N = w.shape[2]
    return pl.pallas_call(
        _out_final_kernel,
        grid=(B, T // tm),
        in_specs=[pl.BlockSpec((1, tm, K), lambda b, i: (b, i, 0)),
                  pl.BlockSpec((1, K, N), lambda b, i: (0, 0, 0), pipeline_mode=pl.Buffered(1)),
                  pl.BlockSpec((1, tm, N), lambda b, i: (b, i, 0)),
                  gate[1],
                  pl.BlockSpec((1, N), lambda b, i: (0, 0))],
        out_specs=pl.BlockSpec((1, tm, N), lambda b, i: (b, i, 0)),
        out_shape=jax.ShapeDtypeStruct((B, T, N), F32),
        compiler_params=_params(("arbitrary", "arbitrary")),
        name="out_proj_final",
    )(a, w, x, gate[0], norm_g.reshape(1, N))


def _retention_head(q, k, v, gate, S, dec, qd, kd, gl, gn):
    qb = q.astype(BF16)
    vb = v.astype(BF16)
    scores = _bdot_nt(qb, k) * dec
    o = _bdot(scores, vb) + _bdot(qb, S) * qd
    kdt = (k * kd).T
    s_new = gl * S + _bdot(kdt, vb)
    mu = jnp.mean(o, axis=-1, keepdims=True)
    d = o - mu
    var = jnp.mean(d * d, axis=-1, keepdims=True)
    on = d * lax.rsqrt(var + EPS) * gn
    return _silu(gate) * on, s_new


def _ret_kernel(q_ref, k_ref, v_ref, g_ref, dec_ref, qd_ref, kd_ref, gl_ref, gn_ref, *rest, lsel):
    o_ref, s_ref = rest[-2:]

    @pl.when(pl.program_id(2) == 0)
    def _():
        s_ref[...] = jnp.zeros_like(s_ref)

    for hd in range(RET_HEADS_PER_STEP):
        kc = slice(hd * RET_DK, (hd + 1) * RET_DK)
        vc = slice(hd * RET_DV, (hd + 1) * RET_DV)
        o, s_new = _retention_head(q_ref[0, :, kc], k_ref[0, :, kc], v_ref[0, :, vc], g_ref[0, :, vc],
                                   s_ref[lsel, 0, hd], dec_ref[hd], qd_ref[hd], kd_ref[hd], gl_ref[hd],
                                   gn_ref[0, :, vc])
        s_ref[lsel, 0, hd] = s_new
        o_ref[0, :, vc] = o.astype(o_ref.dtype)


def _decay_tables(L, Lp):
    log_gamma = jnp.log(1.0 - jnp.exp2(-5.0 - jnp.arange(RET_HEADS, dtype=F32)))
    idx = jnp.arange(L, dtype=F32)
    rel = idx[:, None] - idx[None, :]
    dec = jnp.where(rel >= 0, jnp.exp(jnp.maximum(rel, 0.0)[None] * log_gamma[:, None, None]), 0.0)
    qd = jnp.exp((idx + 1.0)[None, :] * log_gamma[:, None])[..., None]
    kd = jnp.exp((L - 1.0 - idx)[None, :] * log_gamma[:, None])[..., None]
    gl = jnp.exp(L * log_gamma).reshape(RET_HEADS, 1, 1)
    p = Lp - L
    return (jnp.pad(dec, ((0, 0), (0, p), (0, p))), jnp.pad(qd, ((0, 0), (0, p), (0, 0))),
            jnp.pad(kd, ((0, 0), (0, p), (0, 0))), gl)


def _rope_tables_full(pos):
    half = RET_DK // 2
    inv = RET_ROPE_BASE ** (-jnp.arange(half, dtype=F32) * 2.0 / RET_DK)
    ang = pos.astype(F32)[:, None] * inv[None, :]
    return jnp.cos(ang), jnp.sin(ang)


def _chain(prev, in_specs, args, out_index):
    if prev is None:
        return {}
    in_specs.append(pl.BlockSpec(memory_space=pl.ANY))
    args.append(prev)
    return {len(args) - 1: out_index}


def _layer_block(prev, l):
    return (N_A, 0, l) if prev is None else (1, l, 0)


def retention_prompt(qkvg, gn, l, states):
    B, T, _ = qkvg.shape
    L = RET_BLOCK
    H = RET_HEADS
    P = RET_HEADS_PER_STEP
    dec, qd, kd, gl = _decay_tables(L, L)
    qk_blocks = H // P
    v0 = 2 * H * RET_DK // (P * RET_DV)
    in_specs = [pl.BlockSpec((1, L, P * RET_DK), lambda b, h, c: (b, c, h)),
                pl.BlockSpec((1, L, P * RET_DK), lambda b, h, c: (b, c, qk_blocks + h)),
                pl.BlockSpec((1, L, P * RET_DV), lambda b, h, c: (b, c, v0 + h)),
                pl.BlockSpec((1, L, P * RET_DV), lambda b, h, c: (b, c, v0 + qk_blocks + h)),
                pl.BlockSpec((P, L, L), lambda b, h, c: (h, 0, 0)),
                pl.BlockSpec((P, L, 1), lambda b, h, c: (h, 0, 0)),
                pl.BlockSpec((P, L, 1), lambda b, h, c: (h, 0, 0)),
                pl.BlockSpec((P, 1, 1), lambda b, h, c: (h, 0, 0)),
                pl.BlockSpec((1, 1, P * RET_DV), lambda b, h, c: (l, 0, h))]
    args = [qkvg, qkvg, qkvg, qkvg, dec, qd, kd, gl, gn.reshape(gn.shape[0], 1, -1)]
    aliases = _chain(states, in_specs, args, 1)
    nl, lb, lsel = _layer_block(states, l)
    return pl.pallas_call(
        functools.partial(_ret_kernel, lsel=lsel),
        grid=(B, H // P, T // L),
        in_specs=in_specs,
        out_specs=[pl.BlockSpec((1, L, P * RET_DV), lambda b, h, c: (b, c, h)),
                   pl.BlockSpec((nl, 1, P, RET_DK, RET_DV), lambda b, h, c: (lb, b, h, 0, 0))],
        out_shape=[jax.ShapeDtypeStruct((B, T, H * RET_DV), BF16),
                   jax.ShapeDtypeStruct((N_A, B, H, RET_DK, RET_DV), F32)],
        input_output_aliases=aliases,
        compiler_params=_params(("arbitrary", "arbitrary", "arbitrary")),
        name="retention_prompt",
    )(*args)


def _ret_step_kernel(x_ref, s_ref, dec_ref, qd_ref, kd_ref, gl_ref, gn_ref, *rest, seq, lsel):
    o_ref, so_ref = rest[-2:]
    H = RET_HEADS
    zpad = jnp.zeros((RET_PAD - seq, RET_DV), F32)
    for other in range(so_ref.shape[0]):
        if other != lsel:
            so_ref[other] = jnp.zeros(so_ref.shape[1:], F32)

    def padded(col0, width):
        return jnp.concatenate([x_ref[:, col0:col0 + width], zpad[:, :width]], axis=0)

    for h in range(H):
        q = padded(h * RET_DK, RET_DK)
        k = padded(H * RET_DK + h * RET_DK, RET_DK)
        v = padded(2 * H * RET_DK + h * RET_DV, RET_DV)
        gate = padded(2 * H * RET_DK + H * RET_DV + h * RET_DV, RET_DV)
        o, s_new = _retention_head(q, k, v, gate, s_ref[0, 0, h], dec_ref[h], qd_ref[h], kd_ref[h],
                                   gl_ref[h], gn_ref[0, :, h * RET_DV:(h + 1) * RET_DV])
        so_ref[lsel, 0, h] = s_new
        o_ref[:, h * RET_DV:(h + 1) * RET_DV] = o[:seq]


def retention_step(qkvg, state, gn, l, states, *, seq):
    rows, width = qkvg.shape
    nb = rows // seq
    H = RET_HEADS
    dec, qd, kd, gl = _decay_tables(seq, RET_PAD)
    whole = lambda shape: pl.BlockSpec(shape, lambda b: (0,) * len(shape))
    in_specs = [pl.BlockSpec((seq, width), lambda b: (b, 0)),
                pl.BlockSpec((1, 1, H, RET_DK, RET_DV), lambda b: (l, b, 0, 0, 0)),
                whole((H, RET_PAD, RET_PAD)), whole((H, RET_PAD, 1)), whole((H, RET_PAD, 1)), whole((H, 1, 1)),
                pl.BlockSpec((1, 1, H * RET_DV), lambda b: (l, 0, 0))]
    args = [qkvg, state, dec, qd, kd, gl, gn.reshape(gn.shape[0], 1, -1)]
    aliases = _chain(states, in_specs, args, 1)
    nl, lb, lsel = _layer_block(states, l)
    return pl.pallas_call(
        functools.partial(_ret_step_kernel, seq=seq, lsel=lsel),
        grid=(nb,),
        in_specs=in_specs,
        out_specs=[pl.BlockSpec((seq, H * RET_DV), lambda b: (b, 0)),
                   pl.BlockSpec((nl, 1, H, RET_DK, RET_DV), lambda b: (lb, b, 0, 0, 0))],
        out_shape=[jax.ShapeDtypeStruct((rows, H * RET_DV), F32),
                   jax.ShapeDtypeStruct(state.shape, F32)],
        input_output_aliases=aliases,
        compiler_params=_params(("arbitrary",)),
        name="retention_step",
    )(*args)


def _head_pairs(kcat, vcat):
    lo = lax.broadcasted_iota(jnp.int32, (kcat.shape[0], LANES), 1) < HEAD_DIM
    for pair in range(N_KV_HEADS // 2):
        kp = kcat[:, pair * LANES:(pair + 1) * LANES]
        vp = vcat[:, pair * LANES:(pair + 1) * LANES]
        kr = pltpu.roll(kp, HEAD_DIM, axis=1)
        vr = pltpu.roll(vp, HEAD_DIM, axis=1)
        for sub in range(2):
            ka, kb_ = (kp, kr) if sub == 0 else (kr, kp)
            va, vb_ = (vp, vr) if sub == 0 else (vr, vp)
            halves = [(jnp.where(lo, ka, 0.0).astype(BF16), jnp.where(lo, va, 0.0).astype(BF16)),
                      (jnp.where(lo, 0.0, kb_).astype(BF16), jnp.where(lo, 0.0, vb_).astype(BF16))]
            yield pair * 2 + sub, halves


def _sink_softmax(s, ok, sink):
    s = jnp.where(ok, s * (HEAD_DIM ** -0.5), NEG_INF)
    m = jnp.maximum(jnp.max(s, axis=-1, keepdims=True), sink)
    p = jnp.exp(s - m)
    return p * (1.0 / (jnp.sum(p, axis=-1, keepdims=True) + jnp.exp(sink - m)))


def _attend(q_of, kcat, vcat, ok, sink_ref, sink0, rows, store):
    pieces = GQA_GROUPS // 2
    for kh, halves in _head_pairs(kcat, vcat):
        base = kh * GQA_GROUPS * HEAD_DIM
        qs = jnp.concatenate([q_of(base + g * LANES) for g in range(pieces)], axis=0).astype(BF16)
        out = None
        for half, (kk, vv) in enumerate(halves):
            s = _bdot_nt(qs, kk)
            p = jnp.concatenate(
                [_sink_softmax(s[g * rows:(g + 1) * rows], ok, sink_ref[sink0 + kh * GQA_GROUPS + 2 * g + half])
                 for g in range(pieces)], axis=0)
            o = _bdot(p, vv)
            out = o if out is None else out + o
        for g in range(pieces):
            store(base + g * LANES, out[g * rows:(g + 1) * rows])


def _swa_kernel(sink_ref, q_ref, kp_ref, kc_ref, vp_ref, vc_ref, o_ref, *, sink0):
    i = pl.program_id(1)
    kcat = jnp.concatenate([kp_ref[0], kc_ref[0]], axis=0)
    vcat = jnp.concatenate([vp_ref[0], vc_ref[0]], axis=0)
    ql = lax.broadcasted_iota(jnp.int32, (ATT_BLOCK, 2 * ATT_BLOCK), 0)
    km = lax.broadcasted_iota(jnp.int32, (ATT_BLOCK, 2 * ATT_BLOCK), 1)
    no_prev = jnp.where(i > 0, 0, 4 * ATT_BLOCK)
    ok = ((km < ATT_BLOCK) & (km >= ql + no_prev)) | ((km >= ATT_BLOCK) & (km - ATT_BLOCK <= ql))

    def store(col0, val):
        o_ref[0, :, col0:col0 + LANES] = val.astype(o_ref.dtype)

    _attend(lambda c0: q_ref[0, :, c0:c0 + LANES], kcat, vcat, ok, sink_ref, sink0, ATT_BLOCK, store)


def swa_prompt(q, kv, sinks, j):
    B, T, DQ = q.shape
    DKV = kv.shape[2] // 2
    cur = lambda b, i: (b, i, 0)
    return pl.pallas_call(
        functools.partial(_swa_kernel, sink0=j * N_Q_HEADS),
        grid=(B, T // ATT_BLOCK),
        in_specs=[pl.BlockSpec(memory_space=pltpu.SMEM),
                  pl.BlockSpec((1, ATT_BLOCK, DQ), cur),
                  pl.BlockSpec((1, ATT_BLOCK, DKV), lambda b, i: (b, jnp.maximum(i - 1, 0), 0)),
                  pl.BlockSpec((1, ATT_BLOCK, DKV), lambda b, i: (b, i, 0)),
                  pl.BlockSpec((1, ATT_BLOCK, DKV), lambda b, i: (b, jnp.maximum(i - 1, 0), 1)),
                  pl.BlockSpec((1, ATT_BLOCK, DKV), lambda b, i: (b, i, 1))],
        out_specs=pl.BlockSpec((1, ATT_BLOCK, DQ), cur),
        out_shape=jax.ShapeDtypeStruct((B, T, DQ), BF16),
        compiler_params=_params(("arbitrary", "arbitrary")),
        name="swa_prompt",
    )(sinks, q, kv, kv, kv, kv)


def _swa_step_kernel(sink_ref, q_ref, ck_ref, nk_ref, cv_ref, nv_ref, o_ref, *, seq, wb, sink0):
    nk = 2 * ATT_BLOCK
    zpad = jnp.zeros((nk - wb - seq, nk_ref.shape[1]), F32)
    t = lax.broadcasted_iota(jnp.int32, (seq, nk), 0)
    s = lax.broadcasted_iota(jnp.int32, (seq, nk), 1)
    rel = t + wb - s
    ok = (rel >= 0) & (rel <= WINDOW)
    for e in range(ck_ref.shape[0]):
        rows = slice(e * seq, (e + 1) * seq)
        kcat = jnp.concatenate([ck_ref[e], nk_ref[rows, :], zpad], axis=0)
        vcat = jnp.concatenate([cv_ref[e], nv_ref[rows, :], zpad], axis=0)

        def store(col0, val, rows=rows):
            o_ref[rows, col0:col0 + LANES] = val

        _attend(lambda c0, rows=rows: q_ref[rows, c0:c0 + LANES], kcat, vcat, ok, sink_ref, sink0, seq, store)


def swa_step(q, cache_k, cache_v, kv_new, sinks, j, *, seq, per_step=8):
    rows, DQ = q.shape
    nb, wb, DKV = cache_k.shape
    E = per_step
    return pl.pallas_call(
        functools.partial(_swa_step_kernel, seq=seq, wb=wb, sink0=j * N_Q_HEADS),
        grid=(nb // E,),
        in_specs=[pl.BlockSpec(memory_space=pltpu.SMEM),
                  pl.BlockSpec((E * seq, DQ), lambda b: (b, 0)),
                  pl.BlockSpec((E, wb, DKV), lambda b: (b, 0, 0)), pl.BlockSpec((E * seq, DKV), lambda b: (b, 0)),
                  pl.BlockSpec((E, wb, DKV), lambda b: (b, 0, 0)), pl.BlockSpec((E * seq, DKV), lambda b: (b, 1))],
        out_specs=pl.BlockSpec((E * seq, DQ), lambda b: (b, 0)),
        out_shape=jax.ShapeDtypeStruct((rows, DQ), F32),
        compiler_params=_params(("arbitrary",)),
        name="swa_step",
    )(sinks, q, cache_k, kv_new, cache_v, kv_new)


def _rope_tables_partial(pos):
    half = ROPE_DIM // 2
    inv = ROPE_THETA ** (-jnp.arange(half, dtype=F32) * 2.0 / ROPE_DIM)
    ang = pos.astype(F32)[:, None] * inv[None, :]
    cos, sin = jnp.cos(ang), jnp.sin(ang)
    T = pos.shape[0]
    ones = jnp.ones((T, HEAD_DIM - ROPE_DIM), F32)
    zeros = jnp.zeros((T, HEAD_DIM - ROPE_DIM), F32)
    zh = jnp.zeros((T, half), F32)
    c = jnp.concatenate([cos, cos, ones], axis=1)
    s1 = jnp.concatenate([-sin, zh, zeros], axis=1)
    s2 = jnp.concatenate([zh, sin, zeros], axis=1)
    rep = LANES // HEAD_DIM
    return tuple(jnp.tile(a, (1, rep)) for a in (c, s1, s2))


def _final_norm_kernel(x_ref, g_ref, o_ref):
    x = x_ref[0]
    ms = jnp.mean(x * x, axis=-1, keepdims=True)
    o_ref[0] = x * lax.rsqrt(ms + EPS) * g_ref[...]


def final_norm(x, g, tm):
    B, T, D = x.shape
    return pl.pallas_call(
        _final_norm_kernel,
        grid=(B, T // tm),
        in_specs=[pl.BlockSpec((1, tm, D), lambda b, i: (b, i, 0)), pl.BlockSpec((1, D), lambda b, i: (0, 0))],
        out_specs=pl.BlockSpec((1, tm, D), lambda b, i: (b, i, 0)),
        out_shape=jax.ShapeDtypeStruct((B, T, D), F32),
        compiler_params=_params(("arbitrary", "arbitrary")),
        name="final_norm",
    )(x, g.reshape(1, D))


def kernel(x_prompt, x_sample, state_ret, cache_win_k, cache_win_v, state_conv, c_prompt, c_sample,
           w_ada, b_ada, norm_mix, norm_ffn, ret_w_in, ret_gn, ret_w_out,
           kv_norm, kv_w_ada, kv_b_ada, w_kv, att_w_q, att_sinks, att_w_o,
           ffn_w_up, ffn_conv_w, ffn_conv_b, ffn_w_down, norm_f):
    D = D_MODEL
    BP, TP, _ = x_prompt.shape
    BS, TS, _ = x_sample.shape
    RS = BS * TS
    KV = N_KV_HEADS * HEAD_DIM
    QK = RET_HEADS * RET_DK

    c_all = jnp.concatenate([c_sample, c_prompt], axis=0)
    c_all = jnp.pad(c_all, ((0, -c_all.shape[0] % PACK), (0, 0)))
    mods = Mods(ada_mods(c_all, w_ada, b_ada), BS)
    kv_mods = Mods(ada_mods(c_all, kv_w_ada[None], kv_b_ada[None]), BS)
    tile_col = lambda b, i, j: j

    pos_p = jnp.arange(TP, dtype=jnp.int32)
    pos_s = PAST_LEN + jnp.arange(TS, dtype=jnp.int32)
    tabs_p = _rope_tables_partial(pos_p)
    tabs_s = tuple(jnp.tile(a, (BS, 1)) for a in _rope_tables_partial(pos_s))
    full_p = _rope_tables_full(pos_p)
    full_s = tuple(jnp.tile(a, (BS, 1)) for a in _rope_tables_full(pos_s))
    sinks = att_sinks.reshape(-1)
    kv_norm1, w_kv1 = kv_norm[None], w_kv[None]

    x = x_sample.reshape(1, RS, D)
    wb = cache_win_k.shape[1]
    ck = cache_win_k.reshape(BS, wb, KV)
    cv = cache_win_v.reshape(BS, wb, KV)
    ret_s = None
    conv_s = []
    w16 = [dict() for _ in range(DEPTH)]
    for l in range(DEPTH):
        if l == N_A:
            kv_s, w16_kv = proj(x, kv_norm1, 0, kv_mods.sample(0, 0, D), kv_mods.sample(0, 1, D), w_kv1, 0, tm=RS,
                                tn=2 * KV, tok=TS, out_dtype=F32, rope="partial", rope_tabs=tabs_s, rope_cols=KV,
                                emit_w=True)
            kv_s = kv_s[0]
        sh1, sc1, sh2, sc2 = (mods.sample(l, k, D) for k in (0, 1, 3, 4))
        if l < N_A:
            qkvg, w16[l]["in"] = proj(x, norm_mix, l, sh1, sc1, ret_w_in, l, tm=RS, tn=SAMPLE_TN_PROJ, tok=TS,
                                      out_dtype=F32, rope="full", rope_tabs=full_s, rope_cols=2 * QK, scale_from=QK,
                                      emit_w=True)
            a, ret_s = retention_step(qkvg[0], state_ret, ret_gn, l, ret_s, seq=TS)
            w_mix, wl = ret_w_out, l
        else:
            j = l - N_A
            q, w16[l]["in"] = proj(x, norm_mix, l, sh1, sc1, att_w_q, j, tm=RS, tn=SAMPLE_TN_PROJ, tok=TS,
                                   out_dtype=F32, rope="partial", rope_tabs=tabs_s, rope_cols=D, emit_w=True)
            a = swa_step(q[0], ck, cv, kv_s, sinks, j, seq=TS)
            w_mix, wl = att_w_o, j
        tn = SAMPLE_TN_OUT[w_mix.shape[1]]
        x, w16[l]["out"] = out_proj(a[None], w_mix, wl, x, mods.sample(l, 2, tn, tile_col), tm=RS, tn=tn, tok=TS,
                                    emit_w=True)
        z, ua, ub, w16[l]["up_a"], w16[l]["up_b"] = upconv_step(x, norm_ffn, l, sh2, sc2, ffn_w_up, ffn_conv_w,
                                                                ffn_conv_b, state_conv, seq=TS, tn=SAMPLE_TN_UP)
        conv_s.append(jnp.concatenate([ua.reshape(BS, TS, D_FF)[:, TS - (CONV_W - 1):],
                                       ub.reshape(BS, TS, D_FF)[:, TS - (CONV_W - 1):]], axis=-1))
        tn = SAMPLE_TN_OUT[D_FF]
        x, w16[l]["down"] = out_proj(z[None], ffn_w_down, l, x, mods.sample(l, 5, tn, tile_col), tm=RS, tn=tn, tok=TS,
                                     emit_w=True)
    y_sample = final_norm(x, norm_f, RS).reshape(BS, TS, D)

    x = x_prompt
    ret_p = None
    conv_p = []
    for l in range(DEPTH):
        if l == N_A:
            kv_p = proj_prompt(x, kv_norm1, 0, kv_mods, (0, 0, 1), w16_kv, tm=PROMPT_TM, tn=2 * KV, out_dtype=F32,
                               rope="partial", rope_tabs=tabs_p, rope_cols=KV)
        if l < N_A:
            qkvg = proj_prompt(x, norm_mix, l, mods, (l, 0, 1), w16[l]["in"], tm=PROMPT_TM, tn=PROMPT_TN_PROJ,
                               out_dtype=F32, rope="full", rope_tabs=full_p, rope_cols=2 * QK, scale_from=QK)
            a, ret_p = retention_prompt(qkvg, ret_gn, l, ret_p)
        else:
            q = proj_prompt(x, norm_mix, l, mods, (l, 0, 1), w16[l]["in"], tm=PROMPT_TM, tn=PROMPT_TN_PROJ,
                            out_dtype=BF16, rope="partial", rope_tabs=tabs_p, rope_cols=D)
            a = swa_prompt(q, kv_p, sinks, l - N_A)
        x = out_proj(a, w16[l]["out"], 0, x, mods.prompt(l, 2, PROMPT_TN_OUT, tile_col), tm=PROMPT_TM,
                     tn=PROMPT_TN_OUT, tok=None)
        z, st = upconv_prompt(x, norm_ffn, l, mods, (l, 3, 4), w16[l]["up_a"], w16[l]["up_b"], ffn_conv_w, ffn_conv_b,
                              tm=PROMPT_TM, tn=PROMPT_TN_UP)
        conv_p.append(st[:, -1, :, SUBLANES - (CONV_W - 1):, :].transpose(0, 2, 1, 3).reshape(BP, CONV_W - 1, 2 * D_FF))
        if l < DEPTH - 1:
            x = out_proj(z, w16[l]["down"], 0, x, mods.prompt(l, 5, PROMPT_TN_OUT, tile_col), tm=PROMPT_TM,
                         tn=PROMPT_TN_OUT, tok=None)
        else:
            y_prompt = out_proj_final(z, w16[l]["down"], x, mods.prompt(l, 5, D), norm_f, tm=PROMPT_TM_FINAL)
    wp = min(WINDOW, TP)
    win_k_prompt = kv_p[:, -wp:, :KV].reshape(BP, wp, N_KV_HEADS, HEAD_DIM)
    win_v_prompt = kv_p[:, -wp:, KV:].reshape(BP, wp, N_KV_HEADS, HEAD_DIM)
    k_all = jnp.concatenate([ck, kv_s[:, :KV].reshape(BS, TS, KV)], axis=1)
    v_all = jnp.concatenate([cv, kv_s[:, KV:].reshape(BS, TS, KV)], axis=1)
    win_k_sample = k_all[:, -wb:].reshape(BS, wb, N_KV_HEADS, HEAD_DIM)
    win_v_sample = v_all[:, -wb:].reshape(BS, wb, N_KV_HEADS, HEAD_DIM)

    return (y_prompt, y_sample, ret_p, ret_s, win_k_prompt, win_v_prompt,
            win_k_sample, win_v_sample, jnp.stack(conv_p), jnp.stack(conv_s))
```
